```python
import functools
import jax, jax.numpy as jnp
from jax import lax
import numpy as np


D_MODEL = 1024
BATCH = 8
SEQ = 2048
DEPTH = 2
DEC_BATCH = 32
DEC_SEQ = 1
PAST_LEN = 16384
PAGE_SIZE = 128

NUM_META = 16
D_MIX = D_MODEL
A_HEADS = 8
A_HEAD_DIM = D_MIX // 16
A_WIDTH = A_HEADS * A_HEAD_DIM
B_WIDTH = D_MIX // 4
CONF_WIDTH = 31
C_HEADS = 4
C_HEAD_DIM = D_MIX // 16
C_WIDTH = C_HEADS * C_HEAD_DIM
DN_CONV = 4
DN_CHUNK = 64
Q_BLOCK = 128
D_FF = ((8 * D_MODEL + 3 * 256 - 1) // (3 * 256)) * 256
EPS = 1e-6
FORGET_BIAS = 7.0
CACHE_FORGET_LOGIT = 9.0
IN_SIZES = (A_WIDTH, A_WIDTH, A_WIDTH, A_HEADS, 2 * B_WIDTH, 3 * C_WIDTH, C_HEADS, C_HEADS, C_WIDTH)
IN_COLS = sum(IN_SIZES)
IN_SPLITS = [int(i) for i in np.cumsum(IN_SIZES)[:-1]]
F_START = 3 * A_WIDTH

kernel_name = "hymba_fox_conformer_gdn_step"


def rmsnorm(x, g):
    xf = x.astype(jnp.float32)
    y = xf * lax.rsqrt(jnp.mean(xf * xf, axis=-1, keepdims=True) + EPS)
    return (y * g.astype(jnp.float32)).astype(x.dtype)


def layernorm(x, g, b):
    xf = x.astype(jnp.float32)
    mu = jnp.mean(xf, axis=-1, keepdims=True)
    var = jnp.mean(jnp.square(xf - mu), axis=-1, keepdims=True)
    y = (xf - mu) * lax.rsqrt(var + EPS) * g.astype(jnp.float32) + b.astype(jnp.float32)
    return y.astype(x.dtype)


def l2norm(x):
    xf = x.astype(jnp.float32)
    return xf * lax.rsqrt(jnp.sum(xf * xf, axis=-1, keepdims=True) + EPS)


def causal_dwconv(x, w, buf):
    xp = jnp.concatenate([buf.astype(x.dtype), x], axis=1)
    y = lax.conv_general_dilated(xp, w[:, None, :].astype(x.dtype), window_strides=(1,), padding='VALID',
                                 dimension_numbers=('NWC', 'WIO', 'NWC'), feature_group_count=x.shape[-1])
    return y, xp[:, xp.shape[1] - (w.shape[0] - 1):]


def fox_prompt_attention(q, k, v, logf):
    B, L, H, Dh = q.shape
    scale = Dh ** -0.5
    c = jnp.cumsum(logf, axis=1).transpose(0, 2, 1)
    pos = jnp.arange(L)

    def attend(qb, cb, pb):
        s = jnp.einsum('bqhd,bkhd->bhqk', qb, k).astype(jnp.float32) * scale
        s = s + cb[..., :, None] - c[..., None, :]
        s = jnp.where(pb[:, None] >= pos[None, :], s, -jnp.inf)
        p = jax.nn.softmax(s, axis=-1).astype(v.dtype)
        return jnp.einsum('bhqk,bkhd->bqhd', p, v)

    o_meta = attend(q[:, :NUM_META], c[:, :, :NUM_META], pos[:NUM_META])
    nb = (L - NUM_META) // Q_BLOCK
    qb = q[:, NUM_META:].reshape(B, nb, Q_BLOCK, H, Dh).transpose(1, 0, 2, 3, 4)
    cb = c[:, :, NUM_META:].reshape(B, H, nb, Q_BLOCK).transpose(2, 0, 1, 3)
    pb = pos[NUM_META:].reshape(nb, Q_BLOCK)
    o_real = lax.map(lambda a: attend(a[0], a[1], a[2]), (qb, cb, pb))
    o_real = o_real.transpose(1, 0, 2, 3, 4).reshape(B, L - NUM_META, H, Dh)
    return jnp.concatenate([o_meta, o_real], axis=1)


def fox_sample_attention(q, k, v, logf, k_past, v_past, logf_past):
    T = q.shape[1]
    P = k_past.shape[1]
    scale = q.shape[-1] ** -0.5
    lp = logf_past.astype(jnp.float32)
    r_past = (lax.cumsum(lp, axis=1, reverse=True) - lp).transpose(0, 2, 1)
    c_new = jnp.cumsum(logf, axis=1).transpose(0, 2, 1)
    s_past = jnp.einsum('bthd,bphd->bhtp', q, k_past).astype(jnp.float32) * scale
    s_past = s_past + c_new[..., :, None] + r_past[..., None, :]
    s_new = jnp.einsum('bthd,bshd->bhts', q, k).astype(jnp.float32) * scale
    s_new = s_new + c_new[..., :, None] - c_new[..., None, :]
    s_new = jnp.where(jnp.tril(jnp.ones((T, T), bool)), s_new, -jnp.inf)
    p = jax.nn.softmax(jnp.concatenate([s_past, s_new], axis=-1), axis=-1).astype(v.dtype)
    return (jnp.einsum('bhtp,bphd->bthd', p[..., :P], v_past)
            + jnp.einsum('bhts,bshd->bthd', p[..., P:], v))


def gated_delta_chunked(q, k, v, g, beta, s0, chunk):
    f32 = jnp.float32
    B, L, H, _ = q.shape
    Dv = v.shape[-1]
    n = L // chunk

    def blk(t):
        return t.astype(f32).reshape(B, n, chunk, H, -1).transpose(1, 0, 3, 2, 4)

    qc, kc, vc = blk(q), blk(k), blk(v)
    gc = jnp.cumsum(blk(g[..., None])[..., 0], axis=-1)
    bc = blk(beta[..., None])[..., 0]
    tri = jnp.tril(jnp.ones((chunk, chunk), bool))
    strict = jnp.tril(jnp.ones((chunk, chunk), bool), -1)
    diff = gc[..., :, None] - gc[..., None, :]
    decay = jnp.where(tri, jnp.exp(jnp.where(tri, diff, 0.0)), 0.0)
    kb = kc * bc[..., None]
    a_mat = jnp.eye(chunk, dtype=f32) + jnp.where(strict, jnp.einsum('nbhid,nbhjd->nbhij', kb, kc) * decay, 0.0)
    rhs = jnp.concatenate([vc * bc[..., None], kb * jnp.exp(gc)[..., None]], axis=-1)
    sol = lax.linalg.triangular_solve(a_mat, rhs, left_side=True, lower=True, unit_diagonal=True)
    u0, w = sol[..., :Dv], sol[..., Dv:]
    qk = jnp.einsum('nbhid,nbhjd->nbhij', qc, kc) * decay
    q_dec = qc * jnp.exp(gc)[..., None]
    k_dec = kc * jnp.exp(gc[..., -1:] - gc)[..., None]
    g_last = jnp.exp(gc[..., -1])[..., None, None]

    def step(S, xs):
        u0_i, w_i, qk_i, qd_i, kd_i, gl_i = xs
        u = u0_i - jnp.einsum('bhck,bhkv->bhcv', w_i, S)
        o = jnp.einsum('bhck,bhkv->bhcv', qd_i, S) + jnp.einsum('bhij,bhjv->bhiv', qk_i, u)
        S = S * gl_i + jnp.einsum('bhck,bhcv->bhkv', kd_i, u)
        return S, o

    s_fin, o = lax.scan(step, s0.astype(f32), (u0, w, qk, q_dec, k_dec, g_last))
    o = o.transpose(1, 0, 3, 2, 4).reshape(B, L, H, Dv)
    return o, s_fin


def dn_prompt(q, k, v, g, beta):
    s0 = jnp.zeros((q.shape[0], C_HEADS, C_HEAD_DIM, C_HEAD_DIM), jnp.float32)
    m = NUM_META
    o_m, s_m = gated_delta_chunked(q[:, :m], k[:, :m], v[:, :m], g[:, :m], beta[:, :m], s0, m)
    o_r, s_r = gated_delta_chunked(q[:, m:], k[:, m:], v[:, m:], g[:, m:], beta[:, m:], s_m, DN_CHUNK)
    return jnp.concatenate([o_m, o_r], axis=1), s_r


def mixer(xn, lp, attn_fn, conf_buf, dn_buf, dn_fn):
    B, T, _ = xn.shape
    proj = xn @ lp['w_in'] + lp['b_in']
    q_a, k_a, v_a, f_a, glu_in, qkv_c, a_c, b_c, z_c = jnp.split(proj, IN_SPLITS, axis=-1)
    q_a = q_a.reshape(B, T, A_HEADS, A_HEAD_DIM)
    k_a = k_a.reshape(B, T, A_HEADS, A_HEAD_DIM)
    v_a = v_a.reshape(B, T, A_HEADS, A_HEAD_DIM)
    logf = jax.nn.log_sigmoid(f_a.astype(jnp.float32))
    o_a = attn_fn(q_a, k_a, v_a, logf).reshape(B, T, A_WIDTH)
    val, gate = jnp.split(glu_in, 2, axis=-1)
    h = val * jax.nn.sigmoid(gate)
    hc, new_conf_buf = causal_dwconv(h, lp['conf_dw'], conf_buf)
    hc = layernorm(hc + lp['conf_dw_b'], lp['conf_ln_g'], lp['conf_ln_b'])
    o_b = jax.nn.silu(hc) @ lp['conf_pw'] + lp['conf_pw_b']
    qkv, new_dn_buf = causal_dwconv(qkv_c, lp['dn_conv'], dn_buf)
    q_c, k_c, v_c = jnp.split(jax.nn.silu(qkv), 3, axis=-1)
    q_c = l2norm(q_c.reshape(B, T, C_HEADS, C_HEAD_DIM)) * (C_HEAD_DIM ** -0.5)
    k_c = l2norm(k_c.reshape(B, T, C_HEADS, C_HEAD_DIM))
    v_c = v_c.reshape(B, T, C_HEADS, C_HEAD_DIM)
    g = -jnp.exp(lp['dn_a_log'].astype(jnp.float32)) * jax.nn.softplus(
        a_c.astype(jnp.float32) + lp['dn_dt_bias'].astype(jnp.float32))
    beta = jax.nn.sigmoid(b_c.astype(jnp.float32))
    o_c, new_s = dn_fn(q_c, k_c, v_c, g, beta)
    o_c = rmsnorm(o_c.astype(xn.dtype), lp['dn_norm_g']) * jax.nn.silu(z_c.reshape(B, T, C_HEADS, C_HEAD_DIM))
    mixed = jnp.concatenate([rmsnorm(o_a, lp['grp_norm_a']), rmsnorm(o_b, lp['grp_norm_b']),
                             o_c.reshape(B, T, C_WIDTH)], axis=-1)
    return mixed @ lp['w_out'], (k_a, v_a, logf, new_conf_buf, new_dn_buf, new_s)


def swiglu(x, wg, wu, wd):
    return (jax.nn.silu(x @ wg) * (x @ wu)) @ wd


def setup_inputs(seed: int = 0) -> dict:
    key = jax.random.key(seed)
    ks = jax.random.split(key, 40)
    f32 = jnp.float32
    n_pages = PAST_LEN // PAGE_SIZE
    n_used = DEC_BATCH * n_pages
    n_pool = n_used + max(1, n_used // 4)

    def nrm(k, shape, scale):
        return jax.random.normal(k, shape, f32) * scale

    def gain(k, shape):
        return 1.0 + 0.1 * jax.random.normal(k, shape, f32)

    page_table = jax.random.permutation(ks[0], n_pool)[:n_used].reshape(DEC_BATCH, n_pages).astype(jnp.int32)
    dt = jnp.exp(jax.random.uniform(ks[1], (DEPTH, C_HEADS), f32, np.log(1e-3), np.log(1e-1)))
    b_in = nrm(ks[13], (DEPTH, IN_COLS), 0.02).at[:, F_START:F_START + A_HEADS].add(FORGET_BIAS)
    return {
        'x_prompt': nrm(ks[2], (BATCH, SEQ, D_MODEL), 1.0),
        'x_sample': nrm(ks[3], (DEC_BATCH, DEC_SEQ, D_MODEL), 1.0),
        'cache_k': nrm(ks[4], (DEPTH, n_pool, PAGE_SIZE, A_HEADS, A_HEAD_DIM), 1.0),
        'cache_v': nrm(ks[5], (DEPTH, n_pool, PAGE_SIZE, A_HEADS, A_HEAD_DIM), 1.0),
        'cache_logf': jax.nn.log_sigmoid(nrm(ks[6], (DEPTH, n_pool, PAGE_SIZE, A_HEADS), 0.5) + CACHE_FORGET_LOGIT),
        'state_conf_conv': nrm(ks[7], (DEPTH, DEC_BATCH, CONF_WIDTH - 1, B_WIDTH), 1.0),
        'state_dn_conv': nrm(ks[8], (DEPTH, DEC_BATCH, DN_CONV - 1, 3 * C_WIDTH), 1.0),
        'state_dn_rec': nrm(ks[9], (DEPTH, DEC_BATCH, C_HEADS, C_HEAD_DIM, C_HEAD_DIM), 0.1),
        'page_table': page_table,
        'meta_tokens': nrm(ks[10], (NUM_META, D_MODEL), 1.0),
        'norm_mix': gain(ks[11], (DEPTH, D_MODEL)),
        'w_in': nrm(ks[12], (DEPTH, D_MODEL, IN_COLS), D_MODEL ** -0.5),
        'b_in': b_in,
        'conf_dw': nrm(ks[14], (DEPTH, CONF_WIDTH, B_WIDTH), CONF_WIDTH ** -0.5),
        'conf_dw_b': nrm(ks[15], (DEPTH, B_WIDTH), 0.02),
        'conf_ln_g': gain(ks[16], (DEPTH, B_WIDTH)),
        'conf_ln_b': nrm(ks[17], (DEPTH, B_WIDTH), 0.02),
        'conf_pw': nrm(ks[18], (DEPTH, B_WIDTH, B_WIDTH), B_WIDTH ** -0.5),
        'conf_pw_b': nrm(ks[19], (DEPTH, B_WIDTH), 0.02),
        'dn_conv': nrm(ks[20], (DEPTH, DN_CONV, 3 * C_WIDTH), DN_CONV ** -0.5),
        'dn_a_log': jnp.log(jax.random.uniform(ks[21], (DEPTH, C_HEADS), f32, 1.0, 16.0)),
        'dn_dt_bias': dt + jnp.log(-jnp.expm1(-dt)),
        'dn_norm_g': gain(ks[22], (DEPTH, C_HEAD_DIM)),
        'grp_norm_a': gain(ks[23], (DEPTH, A_WIDTH)),
        'grp_norm_b': gain(ks[24], (DEPTH, B_WIDTH)),
        'w_out': nrm(ks[25], (DEPTH, D_MIX, D_MODEL), D_MIX ** -0.5),
        'norm_ffn': gain(ks[26], (DEPTH, D_MODEL)),
        'w_ffn_gate': nrm(ks[27], (DEPTH, D_MODEL, D_FF), D_MODEL ** -0.5),
        'w_ffn_up': nrm(ks[28], (DEPTH, D_MODEL, D_FF), D_MODEL ** -0.5),
        'w_ffn_down': nrm(ks[29], (DEPTH, D_FF, D_MODEL), D_FF ** -0.5),
        'final_norm': gain(ks[30], (D_MODEL,)),
    }


def reference(x_prompt, x_sample, cache_k, cache_v, cache_logf, state_conf_conv, state_dn_conv, state_dn_rec,
              page_table, meta_tokens, norm_mix, w_in, b_in, conf_dw, conf_dw_b, conf_ln_g, conf_ln_b, conf_pw,
              conf_pw_b, dn_conv, dn_a_log, dn_dt_bias, dn_norm_g, grp_norm_a, grp_norm_b, w_out, norm_ffn,
              w_ffn_gate, w_ffn_up, w_ffn_down, final_norm):
    B = x_prompt.shape[0]
    DB, T = x_sample.shape[0], x_sample.shape[1]
    past = page_table.shape[1] * cache_k.shape[2]
    xp = jnp.concatenate([jnp.broadcast_to(meta_tokens.astype(x_prompt.dtype)[None], (B, NUM_META, D_MODEL)),
                          x_prompt], axis=1)
    xs = x_sample
    zero_conf = jnp.zeros((B, CONF_WIDTH - 1, B_WIDTH), x_prompt.dtype)
    zero_dn = jnp.zeros((B, DN_CONV - 1, 3 * C_WIDTH), x_prompt.dtype)
    p_st = [[] for _ in range(6)]
    s_st = [[] for _ in range(6)]
    for l in range(DEPTH):
        lp = {'w_in': w_in[l], 'b_in': b_in[l], 'conf_dw': conf_dw[l], 'conf_dw_b': conf_dw_b[l],
              'conf_ln_g': conf_ln_g[l], 'conf_ln_b': conf_ln_b[l], 'conf_pw': conf_pw[l], 'conf_pw_b': conf_pw_b[l],
              'dn_conv': dn_conv[l], 'dn_a_log': dn_a_log[l], 'dn_dt_bias': dn_dt_bias[l], 'dn_norm_g': dn_norm_g[l],
              'grp_norm_a': grp_norm_a[l], 'grp_norm_b': grp_norm_b[l], 'w_out': w_out[l]}
        y, st = mixer(rmsnorm(xp, norm_mix[l]), lp, fox_prompt_attention, zero_conf, zero_dn, dn_prompt)
        xp = xp + y
        xp = xp + swiglu(rmsnorm(xp, norm_ffn[l]), w_ffn_gate[l], w_ffn_up[l], w_ffn_down[l])
        for i in range(6):
            p_st[i].append(st[i])
        k_past = cache_k[l][page_table].reshape(DB, past, A_HEADS, A_HEAD_DIM)
        v_past = cache_v[l][page_table].reshape(DB, past, A_HEADS, A_HEAD_DIM)
        logf_past = cache_logf[l][page_table].reshape(DB, past, A_HEADS)
        attn_s = functools.partial(fox_sample_attention, k_past=k_past, v_past=v_past, logf_past=logf_past)
        dn_s = functools.partial(gated_delta_chunked, s0=state_dn_rec[l], chunk=T)
        y, st = mixer(rmsnorm(xs, norm_mix[l]), lp, attn_s, state_conf_conv[l], state_dn_conv[l], dn_s)
        xs = xs + y
        xs = xs + swiglu(rmsnorm(xs, norm_ffn[l]), w_ffn_gate[l], w_ffn_up[l], w_ffn_down[l])
        for i in range(6):
            s_st[i].append(st[i])
    y_prompt = rmsnorm(xp, final_norm)[:, NUM_META:]
    y_sample = rmsnorm(xs, final_norm)
    p_k, p_v, p_logf, p_conf, p_dnc, p_dns = [jnp.stack(a) for a in p_st]
    s_k, s_v, s_logf, s_conf, s_dnc, s_dns = [jnp.stack(a) for a in s_st]
    return (y_prompt, y_sample, p_k, p_v, p_logf, p_conf, p_dnc, p_dns, s_k, s_v, s_logf, s_conf, s_dnc, s_dns)
```

```python
import functools

import jax
import jax.numpy as jnp
from jax import lax
from jax.experimental import pallas as pl
from jax.experimental.pallas import tpu as pltpu

F32 = jnp.float32
BF16 = jnp.bfloat16
HIGHEST = lax.Precision.HIGHEST

LANES = 128
SUBLANES = 8
VMEM_LIMIT_BYTES = 56 * 1024 * 1024

NUM_META = 16
A_HEADS = 8
HEAD_DIM = 64
A_WIDTH = A_HEADS * HEAD_DIM
B_WIDTH = 256
CONF_WIDTH = 31
C_HEADS = 4
C_WIDTH = C_HEADS * HEAD_DIM
DN_CONV = 4
DN_CHUNK = 64
EPS = 1e-6
NEG_BIG = -1e30

_Q0, _K0, _V0, _GLU0, _QKVC0, _Z0, _SM0 = 0, 512, 1024, 1536, 2048, 2816, 3072
IN_COLS_PAD = 3200
SM_F, SM_A, SM_B = 0, 8, 12


def _cparams(sem):
    return pltpu.CompilerParams(dimension_semantics=sem, vmem_limit_bytes=VMEM_LIMIT_BYTES)


def _const_spec(shape):
    nd = len(shape)
    return pl.BlockSpec(shape, lambda *_: (0,) * nd, pipeline_mode=pl.Buffered(1))


def _dot(a, b, precision=None):
    return jnp.dot(a, b, preferred_element_type=F32, precision=precision)


def _dot_nt(a, b, precision=None):
    return lax.dot_general(a, b, (((1,), (1,)), ((), ())), preferred_element_type=F32, precision=precision)


def _sigmoid(x):
    return 1.0 / (1.0 + jnp.exp(-x))


def _silu(x):
    return x * _sigmoid(x)


def _log_sigmoid(x):
    return jnp.minimum(x, 0.0) - jnp.log1p(jnp.exp(-jnp.abs(x)))


def _softplus(x):
    return jnp.maximum(x, 0.0) + jnp.log1p(jnp.exp(-jnp.abs(x)))


def _rms(x, g):
    return x * lax.rsqrt(jnp.mean(x * x, axis=-1, keepdims=True) + EPS) * g


def _iota(shape, dim):
    return lax.broadcasted_iota(jnp.int32, shape, dim)


def _lane_col(x, lane):
    return jnp.sum(jnp.where(_iota((1, x.shape[1]), 1) == lane, x, 0.0), axis=1, keepdims=True)


def _inproj_kernel(x_ref, g_ref, w_ref, b_ref, qb_ref, kb_ref, vb_ref, qf_ref, kf_ref, vf_ref,
                   h_ref, qkvc_ref, z_ref, small_ref):
    xn = _rms(x_ref[...], g_ref[...]).astype(BF16)

    def seg(lo, hi):
        return _dot(xn, w_ref[:, lo:hi]) + b_ref[:, lo:hi]

    q = seg(_Q0, _K0) * (HEAD_DIM ** -0.5)
    k = seg(_K0, _V0)
    v = seg(_V0, _GLU0)
    qf_ref[...] = q
    kf_ref[...] = k
    vf_ref[...] = v
    for p in range(A_WIDTH // LANES):
        sl = slice(p * LANES, (p + 1) * LANES)
        qb_ref[p] = q[:, sl].astype(BF16)
        kb_ref[p] = k[:, sl].astype(BF16)
        vb_ref[p] = v[:, sl].astype(BF16)
    glu = seg(_GLU0, _QKVC0)
    h_ref[...] = glu[:, :B_WIDTH] * _sigmoid(glu[:, B_WIDTH:])
    qkvc_ref[...] = seg(_QKVC0, _Z0)
    z_ref[...] = seg(_Z0, _SM0)
    small_ref[...] = seg(_SM0, IN_COLS_PAD)


def _inproj(x, g, w, b, tm):
    n, d = x.shape
    row = lambda width: pl.BlockSpec((tm, width), lambda i: (i, 0))
    pair = pl.BlockSpec((A_WIDTH // LANES, tm, LANES), lambda i: (0, i, 0))
    pair_shape = jax.ShapeDtypeStruct((A_WIDTH // LANES, n, LANES), BF16)
    f = lambda width: jax.ShapeDtypeStruct((n, width), F32)
    return pl.pallas_call(
        _inproj_kernel,
        grid=(n // tm,),
        in_specs=[row(d), _const_spec((1, d)), _const_spec(w.shape), _const_spec(b.shape)],
        out_specs=[pair, pair, pair, row(A_WIDTH), row(A_WIDTH), row(A_WIDTH), row(B_WIDTH),
                   row(3 * C_WIDTH), row(C_WIDTH), row(LANES)],
        out_shape=[pair_shape, pair_shape, pair_shape, f(A_WIDTH), f(A_WIDTH), f(A_WIDTH), f(B_WIDTH),
                   f(3 * C_WIDTH), f(C_WIDTH), f(LANES)],
        compiler_params=_cparams(("parallel",)),
        name="inproj",
    )(x, g, w, b)


def _fox_gates_kernel(small_ref, logf_ref, ccol_ref, crow_ref, *, seq):
    lf = _log_sigmoid(small_ref[...])
    logf_ref[...] = lf[:, SM_F:SM_F + A_HEADS]
    tril = (_iota((LANES, LANES), 0) >= _iota((LANES, LANES), 1)).astype(F32)
    pad = LANES - NUM_META
    first = jnp.concatenate([jnp.zeros((pad, LANES), F32), lf[:NUM_META]], axis=0)
    cs = _dot(tril, first, HIGHEST)
    def put_rows(cs, lane0):
        cst = cs.T
        for h in range(A_HEADS):
            crow_ref[h, :, lane0:lane0 + LANES] = cst[h:h + 1]

    ccol_ref[0:NUM_META, :] = cs[pad:]
    put_rows(cs, 0)
    carry = cs[LANES - 1:LANES]
    for i in range((seq - NUM_META) // LANES):
        r0 = NUM_META + i * LANES
        cs = _dot(tril, lf[r0:r0 + LANES], HIGHEST) + carry
        ccol_ref[r0:r0 + LANES, :] = cs
        put_rows(cs, (i + 1) * LANES)
        carry = cs[LANES - 1:LANES]


def _fox_gates(small, batch, seq):
    lrow = seq - NUM_META + LANES
    return pl.pallas_call(
        functools.partial(_fox_gates_kernel, seq=seq),
        grid=(batch,),
        in_specs=[pl.BlockSpec((seq, LANES), lambda b: (b, 0))],
        out_specs=[pl.BlockSpec((None, seq, A_HEADS), lambda b: (b, 0, 0)),
                   pl.BlockSpec((seq, LANES), lambda b: (b, 0)),
                   pl.BlockSpec((None, A_HEADS, 1, lrow), lambda b: (b, 0, 0, 0))],
        out_shape=[jax.ShapeDtypeStruct((batch, seq, A_HEADS), F32),
                   jax.ShapeDtypeStruct((batch * seq, LANES), F32),
                   jax.ShapeDtypeStruct((batch, A_HEADS, 1, lrow), F32)],
        compiler_params=_cparams(("parallel",)),
        name="fox_gates",
    )(small)


def _fox_prompt_kernel(q_ref, k_ref, v_ref, ccol_ref, crow_ref, o_ref, *, seq):
    hp = pl.program_id(1)
    lo = _iota((1, LANES), 1) < HEAD_DIM
    zero = jnp.zeros((), BF16)
    pad = LANES - NUM_META

    def head_masks(x):
        return jnp.where(lo, x, zero), jnp.where(lo, zero, x)

    def cq_cols(rows):
        cc = ccol_ref[rows, :]
        return [_lane_col(cc, 2 * hp + h) for h in range(2)]

    def ck_row(h, lanes):
        return crow_ref[h, :, lanes]

    def update(state, qh, cq, kc, vc, ck, mask):
        m, l, acc = state
        vh = head_masks(vc)
        alphas, pv = [], None
        m_new, l_new = [], []
        for h in range(2):
            s = _dot_nt(qh[h], kc) + cq[h] - ck[h]
            if mask is not None:
                s = jnp.where(mask, s, NEG_BIG)
            mh = jnp.maximum(m[h], jnp.max(s, axis=1, keepdims=True))
            a = jnp.exp(m[h] - mh)
            p = jnp.exp(s - mh)
            m_new.append(mh)
            l_new.append(a * l[h] + jnp.sum(p, axis=1, keepdims=True))
            alphas.append(a)
            d = _dot(p.astype(BF16), vh[h])
            pv = d if pv is None else pv + d
        acc = acc * jnp.where(lo, alphas[0], alphas[1]) + pv
        return m_new, l_new, acc

    def finish(state):
        m, l, acc = state
        return acc / jnp.where(lo, l[0], l[1])

    def init(nq):
        neg = jnp.full((nq, 1), NEG_BIG, F32)
        zer = jnp.zeros((nq, 1), F32)
        return [neg, neg], [zer, zer], jnp.zeros((nq, LANES), F32)

    def tril(n):
        return _iota((n, n), 0) >= _iota((n, n), 1)

    kmeta, vmeta = k_ref[0:NUM_META, :], v_ref[0:NUM_META, :]
    ck_meta = [ck_row(h, slice(0, LANES))[:, pad:] for h in range(2)]
    st = update(init(NUM_META), head_masks(q_ref[0:NUM_META, :]), cq_cols(slice(0, NUM_META)),
                kmeta, vmeta, ck_meta, tril(NUM_META))
    o_ref[0:NUM_META, :] = finish(st)

    diag = tril(LANES)

    def q_block(i, carry):
        r0 = pl.multiple_of(NUM_META + i * LANES, NUM_META)
        rows = pl.ds(r0, LANES)
        qh = head_masks(q_ref[rows, :])
        cq = cq_cols(rows)
        st = update(init(LANES), qh, cq, kmeta, vmeta, ck_meta, None)

        def kv_chunk(j, st, mask=None):
            c0 = pl.multiple_of(NUM_META + j * LANES, NUM_META)
            lanes = pl.ds(pl.multiple_of((j + 1) * LANES, LANES), LANES)
            ck = [ck_row(h, lanes) for h in range(2)]
            return update(st, qh, cq, k_ref[pl.ds(c0, LANES), :], v_ref[pl.ds(c0, LANES), :], ck, mask)

        def body(j, flat):
            m0, m1, l0, l1, acc = flat
            m, l, acc = kv_chunk(j, ([m0, m1], [l0, l1], acc))
            return m[0], m[1], l[0], l[1], acc

        m, l, acc = st
        m0, m1, l0, l1, acc = lax.fori_loop(0, i, body, (m[0], m[1], l[0], l[1], acc))
        st = kv_chunk(i, ([m0, m1], [l0, l1], acc), diag)
        o_ref[rows, :] = finish(st)
        return carry

    lax.fori_loop(0, (seq - NUM_META) // LANES, q_block, 0)


def _fox_prompt(qb, kb, vb, ccol, crow, batch, seq):
    n = batch * seq
    pairs = A_WIDTH // LANES
    qkv = pl.BlockSpec((None, seq, LANES), lambda b, p: (p, b, 0))
    return pl.pallas_call(
        functools.partial(_fox_prompt_kernel, seq=seq),
        grid=(batch, pairs),
        in_specs=[qkv, qkv, qkv,
                  pl.BlockSpec((seq, LANES), lambda b, p: (b, 0)),
                  pl.BlockSpec((None, 2, 1, crow.shape[-1]), lambda b, p: (b, p, 0, 0))],
        out_specs=pl.BlockSpec((None, seq, LANES), lambda b, p: (p, b, 0)),
        out_shape=jax.ShapeDtypeStruct((pairs, n, LANES), F32),
        compiler_params=_cparams(("parallel", "parallel")),
        name="fox_prompt",
    )(qb, kb, vb, ccol, crow)


CONF_ROWS = 48
CONF_PAD = 32


def _conformer_kernel(h_ref, dw_ref, dwb_ref, lng_ref, lnb_ref, pw_ref, pwb_ref, ob_ref, hp_ref, *, seq):
    hp_ref[0:CONF_PAD, :] = jnp.zeros((CONF_PAD, B_WIDTH), F32)
    hp_ref[CONF_PAD:CONF_PAD + seq, :] = h_ref[...]
    shift = CONF_PAD - (CONF_WIDTH - 1)

    def chunk(c, carry):
        r0 = pl.multiple_of(c * CONF_ROWS, SUBLANES)
        win = hp_ref[pl.ds(r0, CONF_ROWS + CONF_PAD), :]
        acc = jnp.zeros((CONF_ROWS, B_WIDTH), F32) + dwb_ref[...]
        for w in range(CONF_WIDTH):
            acc = acc + dw_ref[w:w + 1, :] * win[shift + w:shift + w + CONF_ROWS, :]
        mu = jnp.mean(acc, axis=-1, keepdims=True)
        xc = acc - mu
        var = jnp.mean(xc * xc, axis=-1, keepdims=True)
        y = xc * lax.rsqrt(var + EPS) * lng_ref[...] + lnb_ref[...]
        ob_ref[pl.ds(r0, CONF_ROWS), :] = _dot(_silu(y).astype(BF16), pw_ref[...]) + pwb_ref[...]
        return carry

    lax.fori_loop(0, seq // CONF_ROWS, chunk, 0)


def _conformer(h, dw, dwb, lng, lnb, pw, pwb, batch, seq):
    assert seq % CONF_ROWS == 0
    vec = _const_spec((1, B_WIDTH))
    return pl.pallas_call(
        functools.partial(_conformer_kernel, seq=seq),
        grid=(batch,),
        in_specs=[pl.BlockSpec((seq, B_WIDTH), lambda b: (b, 0)), _const_spec(dw.shape), vec, vec, vec,
                  _const_spec(pw.shape), vec],
        out_specs=pl.BlockSpec((seq, B_WIDTH), lambda b: (b, 0)),
        out_shape=jax.ShapeDtypeStruct((batch * seq, B_WIDTH), F32),
        scratch_shapes=[pltpu.VMEM((CONF_PAD + seq, B_WIDTH), F32)],
        compiler_params=_cparams(("parallel",)),
        name="conformer",
    )(h, dw, dwb, lng, lnb, pw, pwb)


DN_PAD = 8
DN_PREP_ROWS = 344


def _divisor_rows(seq, cap):
    return max(r for r in range(SUBLANES, cap + 1, SUBLANES) if seq % r == 0)


def _head_sum_matrix():
    return ((_iota((C_WIDTH, C_WIDTH), 0) // HEAD_DIM) == (_iota((C_WIDTH, C_WIDTH), 1) // HEAD_DIM)).astype(F32)


def _head_sumsq(x, ones_bd):
    return _dot(x * x, ones_bd, HIGHEST)


def _unit_lower_inverse(ns, size):
    n = ns.shape[0]
    eye = (_iota((n, n), 0) == _iota((n, n), 1)).astype(F32)
    inv = eye - ns
    pw = ns
    k = 2
    while k < size:
        pw = _dot(pw, pw, HIGHEST)
        inv = inv + _dot(inv, pw, HIGHEST)
        k *= 2
    return inv


def _expand_heads(x):
    head = _iota((1, C_WIDTH), 1) // HEAD_DIM
    return jnp.concatenate([jnp.where(head == h, x, 0.0) for h in range(C_HEADS)], axis=0)


def _delta_chunk(q, k, v, g, bt, s_bd, chunk):
    c = chunk
    n = C_HEADS * c
    tril_c = (_iota((c, c), 0) >= _iota((c, c), 1)).astype(F32)
    gcum = _dot(tril_c, g, HIGHEST)
    gcol = jnp.concatenate([_lane_col(gcum, SM_A + h) for h in range(C_HEADS)], axis=0)
    bcol = jnp.concatenate([_lane_col(bt, SM_B + h) for h in range(C_HEADS)], axis=0)
    glast = jnp.concatenate([jnp.broadcast_to(_lane_col(gcum[c - 1:c], SM_A + h), (c, 1))
                             for h in range(C_HEADS)], axis=0)
    gmask = jnp.concatenate([jnp.where(_iota((1, LANES), 1) == SM_A + h, gcum, 0.0) for h in range(C_HEADS)], axis=0)
    grow = _dot_nt(jnp.ones((n, LANES), F32), gmask, HIGHEST)
    ri, ci = _iota((n, n), 0), _iota((n, n), 1)
    same = (ri // c) == (ci // c)
    tri = same & (ri >= ci)
    strict = same & (ri > ci)
    decay = jnp.where(tri, jnp.exp(jnp.where(tri, gcol - grow, 0.0)), 0.0)

    kx, qx, vx = _expand_heads(k), _expand_heads(q), _expand_heads(v)
    kb = kx * bcol
    kxb = kx.astype(BF16)
    a_strict = jnp.where(strict, _dot_nt(kb.astype(BF16), kxb) * decay, 0.0)
    inv = _unit_lower_inverse(a_strict, c)
    eg = jnp.exp(gcol)
    u0 = _dot(inv, vx * bcol, HIGHEST)
    w = _dot(inv, kb * eg, HIGHEST)
    qk = _dot_nt(qx.astype(BF16), kxb) * decay
    q_dec = qx * eg
    k_dec = kx * jnp.exp(glast - gcol)
    sb = s_bd.astype(BF16)
    u = u0 - _dot(w.astype(BF16), sb)
    o = _dot(q_dec.astype(BF16), sb) + _dot(qk.astype(BF16), u.astype(BF16))
    gl_rows = jnp.concatenate([jnp.broadcast_to(jnp.exp(_lane_col(gcum[c - 1:c], SM_A + h)), (HEAD_DIM, 1))
                               for h in range(C_HEADS)], axis=0)
    s_new = s_bd * gl_rows + _dot(k_dec.T.astype(BF16), u.astype(BF16))
    o_c = o[0:c]
    for h in range(1, C_HEADS):
        o_c = o_c + o[h * c:(h + 1) * c]
    return o_c, s_new


def _deltanet_kernel(x_ref, z_ref, small_ref, cw_ref, alog_ref, dtb_ref, ng_ref, oc_ref, srec_ref,
                     xp_ref, q_ref, k_ref, v_ref, g_ref, bt_ref, *, seq):
    keep = DN_CONV - 1
    xp_ref[0:DN_PAD, :] = jnp.zeros((DN_PAD, 3 * C_WIDTH), F32)
    xp_ref[DN_PAD:DN_PAD + seq, :] = x_ref[...]
    ones_bd = _head_sum_matrix()
    sm = small_ref[...]
    g_ref[...] = -jnp.exp(alog_ref[...]) * _softplus(sm + dtb_ref[...])
    bt_ref[...] = _sigmoid(sm)

    shift = DN_PAD - keep
    prep = _divisor_rows(seq, DN_PREP_ROWS)
    for c in range(seq // prep):
        r0 = c * prep
        acc = None
        for w in range(DN_CONV):
            t = cw_ref[w:w + 1, :] * xp_ref[r0 + shift + w:r0 + shift + w + prep, :]
            acc = t if acc is None else acc + t
        y = _silu(acc)
        qq, kk, vv = y[:, :C_WIDTH], y[:, C_WIDTH:2 * C_WIDTH], y[:, 2 * C_WIDTH:]
        rows = slice(r0, r0 + prep)
        q_ref[rows, :] = qq * lax.rsqrt(_head_sumsq(qq, ones_bd) + EPS) * (HEAD_DIM ** -0.5)
        k_ref[rows, :] = kk * lax.rsqrt(_head_sumsq(kk, ones_bd) + EPS)
        v_ref[rows, :] = vv

    def run_chunk(r0, chunk, s_bd):
        rows = pl.ds(r0, chunk)
        o, s_bd = _delta_chunk(q_ref[rows, :], k_ref[rows, :], v_ref[rows, :], g_ref[rows, :], bt_ref[rows, :],
                               s_bd, chunk)
        ms = _dot(o * o, ones_bd, HIGHEST) * (1.0 / HEAD_DIM)
        oc_ref[rows, :] = o * lax.rsqrt(ms + EPS) * ng_ref[...] * _silu(z_ref[rows, :])
        return s_bd

    s_bd = run_chunk(0, NUM_META, jnp.zeros((C_WIDTH, C_WIDTH), F32))

    def body(i, s_bd):
        return run_chunk(pl.multiple_of(NUM_META + i * DN_CHUNK, SUBLANES), DN_CHUNK, s_bd)

    s_bd = lax.fori_loop(0, (seq - NUM_META) // DN_CHUNK, body, s_bd)
    half = s_bd[:, 0:LANES] + s_bd[:, LANES:2 * LANES]
    fold = half + pltpu.roll(half, HEAD_DIM, axis=1)
    srec_ref[...] = fold[:, 0:HEAD_DIM].reshape(C_HEADS, HEAD_DIM, HEAD_DIM)


def _deltanet(qkvc, z, small, cw, alog, dtb, ng, batch, seq):
    vec = _const_spec((1, LANES))
    rows = lambda width: pl.BlockSpec((seq, width), lambda b: (b, 0))
    scr = lambda width: pltpu.VMEM((seq, width), F32)
    return pl.pallas_call(
        functools.partial(_deltanet_kernel, seq=seq),
        grid=(batch,),
        in_specs=[rows(3 * C_WIDTH), rows(C_WIDTH), rows(LANES), _const_spec(cw.shape), vec, vec,
                  _const_spec((1, C_WIDTH))],
        out_specs=[rows(C_WIDTH),
                   pl.BlockSpec((None, C_HEADS, HEAD_DIM, HEAD_DIM), lambda b: (b, 0, 0, 0))],
        out_shape=[jax.ShapeDtypeStruct((batch * seq, C_WIDTH), F32),
                   jax.ShapeDtypeStruct((batch, C_HEADS, HEAD_DIM, HEAD_DIM), F32)],
        scratch_shapes=[pltpu.VMEM((DN_PAD + seq, 3 * C_WIDTH), F32), scr(C_WIDTH), scr(C_WIDTH), scr(C_WIDTH),
                        scr(LANES), scr(LANES)],
        compiler_params=_cparams(("parallel",)),
        name="deltanet",
    )(qkvc, z, small, cw, alog, dtb, ng)


def _out_ffn_kernel(x_ref, oa_ref, ob_ref, oc_ref, ga_ref, gb_ref, wo_ref, gf_ref, wg_ref, wu_ref, wd_ref,
                    fin_ref, y_ref, *, final):
    oa = jnp.concatenate([oa_ref[p] for p in range(A_WIDTH // LANES)], axis=-1)
    mixed = jnp.concatenate([_rms(oa, ga_ref[...]), _rms(ob_ref[...], gb_ref[...]), oc_ref[...]], axis=-1)
    x1 = x_ref[...] + _dot(mixed.astype(BF16), wo_ref[...])
    xn = _rms(x1, gf_ref[...]).astype(BF16)
    hid = _silu(_dot(xn, wg_ref[...])) * _dot(xn, wu_ref[...])
    x2 = x1 + _dot(hid.astype(BF16), wd_ref[...])
    y_ref[...] = _rms(x2, fin_ref[...]) if final else x2


def _out_ffn(x, oa, ob, oc, ga, gb, wo, gf, wg, wu, wd, fin, tm, final):
    n, d = x.shape
    row = lambda width: pl.BlockSpec((tm, width), lambda i: (i, 0))
    return pl.pallas_call(
        functools.partial(_out_ffn_kernel, final=final),
        grid=(n // tm,),
        in_specs=[row(d), pl.BlockSpec((A_WIDTH // LANES, tm, LANES), lambda i: (0, i, 0)), row(B_WIDTH),
                  row(C_WIDTH), _const_spec(ga.shape), _const_spec(gb.shape), _const_spec(wo.shape),
                  _const_spec(gf.shape), _const_spec(wg.shape), _const_spec(wu.shape), _const_spec(wd.shape),
                  _const_spec(fin.shape)],
        out_specs=row(d),
        out_shape=jax.ShapeDtypeStruct((n, d), F32),
        compiler_params=_cparams(("parallel",)),
        name="out_ffn",
    )(x, oa, ob, oc, ga, gb, wo, gf, wg, wu, wd, fin)


POOL_BLOCK = 256
PAGES_PER_STEP = 8


def _pool_suffix_kernel(lf_ref, out_ref):
    pages, heads, rows = lf_ref.shape
    x = lf_ref[...].reshape(pages * heads, rows)
    later = (_iota((rows, rows), 0) > _iota((rows, rows), 1)).astype(F32)
    excl = _dot(x, later, HIGHEST)
    tot = _dot(x, jnp.ones((rows, rows), F32), HIGHEST)
    out_ref[:, 0:heads, :] = excl.reshape(pages, heads, rows)
    out_ref[:, heads:2 * heads, :] = tot.reshape(pages, heads, rows)


def _pool_suffix(lf_t):
    n_pool, heads, rows = lf_t.shape
    return pl.pallas_call(
        _pool_suffix_kernel,
        grid=(pl.cdiv(n_pool, POOL_BLOCK),),
        in_specs=[pl.BlockSpec((POOL_BLOCK, heads, rows), lambda i: (i, 0, 0))],
        out_specs=pl.BlockSpec((POOL_BLOCK, 2 * heads, rows), lambda i: (i, 0, 0)),
        out_shape=jax.ShapeDtypeStruct((n_pool, 2 * heads, rows), F32),
        compiler_params=_cparams(("parallel",)),
        name="pool_suffix",
    )(lf_t)


def _fox_decode_kernel(pt_ref, q_ref, kn_ref, vn_ref, cn_ref, *rest):
    pps = PAGES_PER_STEP
    k_refs, v_refs, r_refs = rest[0:pps], rest[pps:2 * pps], rest[2 * pps:3 * pps]
    o_ref, m_ref, l_ref, acc_ref, suf_ref = rest[3 * pps:]
    j = pl.program_id(1)
    head_of_lane = _iota((A_HEADS, A_WIDTH), 1) // HEAD_DIM
    own = head_of_lane == _iota((A_HEADS, A_WIDTH), 0)
    qbd = jnp.where(own, q_ref[...], 0.0)

    @pl.when(j == 0)
    def _():
        m_ref[...] = jnp.broadcast_to(jnp.sum(qbd * kn_ref[...], axis=1, keepdims=True), m_ref.shape)
        l_ref[...] = jnp.ones(l_ref.shape, F32)
        acc_ref[...] = jnp.broadcast_to(vn_ref[...], acc_ref.shape)
        suf_ref[...] = jnp.zeros(suf_ref.shape, F32)

    qb = qbd.astype(BF16)
    cn = cn_ref[...]
    m, l, acc, suf = m_ref[...], l_ref[...], acc_ref[...], suf_ref[...]
    for u in range(pps):
        r = r_refs[u][...]
        s = _dot_nt(qb, k_refs[u][...].astype(BF16)) + (cn + suf + r[0:A_HEADS])
        suf = suf + r[A_HEADS:2 * A_HEADS]
        m_new = jnp.maximum(m, jnp.max(s, axis=1, keepdims=True))
        a = jnp.exp(m - m_new)
        p = jnp.exp(s - m_new)
        l = a * l + jnp.sum(p, axis=1, keepdims=True)
        acc = acc * a[:, 0:1] + _dot(p.astype(BF16), v_refs[u][...].astype(BF16))
        m = m_new
    m_ref[...], l_ref[...], acc_ref[...], suf_ref[...] = m, l, acc, suf

    @pl.when(j == pl.num_programs(1) - 1)
    def _():
        o = jnp.where(own, acc / l[:, 0:1], 0.0)
        o_ref[...] = jnp.sum(o, axis=0, keepdims=True)


def _fox_decode(page_table, q, k_new, v_new, logf_new, cache_k, cache_v, suffix):
    db, n_pages = page_table.shape
    pps = PAGES_PER_STEP
    assert n_pages % pps == 0
    steps = n_pages // pps
    rows = cache_k.shape[1]
    row3 = lambda width: pl.BlockSpec((None, 1, width), lambda b, j, pt: (b, 0, 0))

    def page(u, shape):
        return pl.BlockSpec((None,) + shape, lambda b, j, pt: (pt[b, n_pages - 1 - (j * pps + u)], 0, 0))

    grid_spec = pltpu.PrefetchScalarGridSpec(
        num_scalar_prefetch=1,
        grid=(db, steps),
        in_specs=[row3(A_WIDTH), row3(A_WIDTH), row3(A_WIDTH),
                  pl.BlockSpec((None, A_HEADS, LANES), lambda b, j, pt: (b, 0, 0))]
                 + [page(u, (rows, A_WIDTH)) for u in range(pps)]
                 + [page(u, (rows, A_WIDTH)) for u in range(pps)]
                 + [page(u, (2 * A_HEADS, rows)) for u in range(pps)],
        out_specs=row3(A_WIDTH),
        scratch_shapes=[pltpu.VMEM((A_HEADS, LANES), F32), pltpu.VMEM((A_HEADS, LANES), F32),
                        pltpu.VMEM((A_HEADS, A_WIDTH), F32), pltpu.VMEM((A_HEADS, LANES), F32)],
    )
    return pl.pallas_call(
        _fox_decode_kernel,
        grid_spec=grid_spec,
        out_shape=jax.ShapeDtypeStruct((db, 1, A_WIDTH), F32),
        compiler_params=_cparams(("parallel", "arbitrary")),
        name="fox_decode",
    )(page_table, q.reshape(db, 1, A_WIDTH), k_new.reshape(db, 1, A_WIDTH), v_new.reshape(db, 1, A_WIDTH),
      logf_new, *([cache_k] * pps), *([cache_v] * pps), *([suffix] * pps))


def _sample_mix_kernel(h_ref, cbuf_ref, dw_ref, dwb_ref, lng_ref, lnb_ref, pw_ref, pwb_ref,
                       x_ref, dbuf_ref, cw_ref, small_ref, alog_ref, dtb_ref,
                       ob_ref, q_ref, k_ref, v_ref, eg_ref, bt_ref, lf_ref):
    keep = CONF_WIDTH - 1
    h = h_ref[...]
    cbuf = cbuf_ref[...]
    acc = jnp.sum(cbuf * dw_ref[0:keep, :][None], axis=1) + dw_ref[keep:keep + 1, :] * h + dwb_ref[...]
    mu = jnp.mean(acc, axis=-1, keepdims=True)
    xc = acc - mu
    var = jnp.mean(xc * xc, axis=-1, keepdims=True)
    y = xc * lax.rsqrt(var + EPS) * lng_ref[...] + lnb_ref[...]
    ob_ref[...] = _dot(_silu(y).astype(BF16), pw_ref[...]) + pwb_ref[...]

    dk = DN_CONV - 1
    x = x_ref[...]
    dbuf = dbuf_ref[...]
    conv = jnp.sum(dbuf * cw_ref[0:dk, :][None], axis=1) + cw_ref[dk:dk + 1, :] * x
    yv = _silu(conv)
    qq, kk, vv = yv[:, :C_WIDTH], yv[:, C_WIDTH:2 * C_WIDTH], yv[:, 2 * C_WIDTH:]
    ones_bd = _head_sum_matrix()
    q_ref[...] = qq * lax.rsqrt(_head_sumsq(qq, ones_bd) + EPS) * (HEAD_DIM ** -0.5)
    k_ref[...] = kk * lax.rsqrt(_head_sumsq(kk, ones_bd) + EPS)
    v_ref[...] = vv
    sm = small_ref[...]
    eg_ref[...] = jnp.exp(-jnp.exp(alog_ref[...]) * _softplus(sm + dtb_ref[...]))
    bt_ref[...] = _sigmoid(sm)
    lf_ref[...] = _log_sigmoid(sm)


def _sample_mix(h, cbuf, dw, dwb, lng, lnb, pw, pwb, x, dbuf, cw, small, alog, dtb):
    db = h.shape[0]
    f = lambda *shape: jax.ShapeDtypeStruct(shape, F32)
    return pl.pallas_call(
        _sample_mix_kernel,
        out_shape=[f(db, B_WIDTH), f(db, C_WIDTH), f(db, C_WIDTH), f(db, C_WIDTH),
                   f(db, LANES), f(db, LANES), f(db, LANES)],
        compiler_params=pltpu.CompilerParams(vmem_limit_bytes=VMEM_LIMIT_BYTES),
        name="sample_mix",
    )(h, cbuf, dw, dwb, lng, lnb, pw, pwb, x, dbuf, cw, small, alog, dtb)


def _sample_delta_kernel(s_ref, qc_ref, kc_ref, qr_ref, kr_ref, v_ref, eg_ref, bt_ref, z_ref, ng_ref,
                         o_ref, snew_ref):
    sd = s_ref[...] * eg_ref[...]
    ks = jnp.sum(kc_ref[...] * sd, axis=1, keepdims=True)
    u = bt_ref[...] * (v_ref[...] - ks)
    qs = jnp.sum(qc_ref[...] * sd, axis=1, keepdims=True)
    qk = jnp.sum(qr_ref[...] * kr_ref[...], axis=2, keepdims=True)
    o = qs + qk * u
    snew_ref[...] = sd + kc_ref[...] * u
    ms = jnp.mean(o * o, axis=-1, keepdims=True)
    o_ref[...] = o * lax.rsqrt(ms + EPS) * ng_ref[...] * _silu(z_ref[...])


def _sample_delta(s, qn, kn, vn, eg, bt, z, ng):
    db = s.shape[0]
    g = db * C_HEADS
    col = lambda t: t.reshape(g, HEAD_DIM, 1)
    row = lambda t: t.reshape(g, 1, HEAD_DIM)
    gate = lambda t, off: t[:, off:off + C_HEADS].reshape(g, 1, 1)
    o, s_new = pl.pallas_call(
        _sample_delta_kernel,
        out_shape=[jax.ShapeDtypeStruct((g, 1, HEAD_DIM), F32), jax.ShapeDtypeStruct((g, HEAD_DIM, HEAD_DIM), F32)],
        compiler_params=pltpu.CompilerParams(vmem_limit_bytes=VMEM_LIMIT_BYTES),
        name="sample_delta",
    )(s.reshape(g, HEAD_DIM, HEAD_DIM), col(qn), col(kn), row(qn), row(kn), row(vn), gate(eg, SM_A), gate(bt, SM_B),
      row(z), ng.reshape(1, 1, HEAD_DIM))
    return o.reshape(db, C_WIDTH), s_new.reshape(db, C_HEADS, HEAD_DIM, HEAD_DIM)


def _permute_in_cols(t):
    aw, bw, cw = A_WIDTH, B_WIDTH, C_WIDTH
    f0 = 3 * aw
    glu0 = f0 + A_HEADS
    qkv0 = glu0 + 2 * bw
    a0 = qkv0 + 3 * cw
    b0 = a0 + C_HEADS
    z0 = b0 + C_HEADS
    parts = [t[..., 0:f0], t[..., glu0:qkv0], t[..., qkv0:a0], t[..., z0:z0 + cw],
             t[..., f0:glu0], t[..., a0:b0], t[..., b0:z0]]
    out = jnp.concatenate(parts, axis=-1)
    return jnp.pad(out, [(0, 0)] * (t.ndim - 1) + [(0, IN_COLS_PAD - out.shape[-1])])


def _lane_vec(vals, off):
    return jnp.zeros((1, LANES), F32).at[0, off:off + vals.shape[0]].set(vals)


def _row_tile(n):
    for tm in (384, 256, 128, 64, 32, 16, 8):
        if n % tm == 0:
            return tm
    raise ValueError(n)


def kernel(x_prompt, x_sample, cache_k, cache_v, cache_logf, state_conf_conv, state_dn_conv, state_dn_rec, page_table, meta_tokens, norm_mix, w_in, b_in, conf_dw, conf_dw_b, conf_ln_g, conf_ln_b, conf_pw, conf_pw_b, dn_conv, dn_a_log, dn_dt_bias, dn_norm_g, grp_norm_a, grp_norm_b, w_out, norm_ffn, w_ffn_gate, w_ffn_up, w_ffn_down, final_norm):
    batch, seq_in, d = x_prompt.shape
    db = x_sample.shape[0]
    assert x_sample.shape[1] == 1
    depth = w_in.shape[0]
    seq = NUM_META + seq_in
    n = batch * seq
    n_pool, page_rows = cache_k.shape[1], cache_k.shape[2]

    xp = jnp.concatenate([jnp.broadcast_to(meta_tokens[None], (batch, NUM_META, d)), x_prompt], axis=1).reshape(n, d)
    xs = x_sample.reshape(db, d)
    w_in_p = _permute_in_cols(w_in).astype(BF16)
    b_in_p = _permute_in_cols(b_in)
    tm_p, tm_s = _row_tile(n), _row_tile(db)
    fin = final_norm.reshape(1, d)
    r1 = lambda t: t.reshape(1, -1)

    p_out = [[] for _ in range(6)]
    s_out = [[] for _ in range(6)]
    for l in range(depth):
        g_mix, g_ffn = r1(norm_mix[l]), r1(norm_ffn[l])
        wo, wg, wu, wd = (w_out[l].astype(BF16), w_ffn_gate[l].astype(BF16), w_ffn_up[l].astype(BF16),
                          w_ffn_down[l].astype(BF16))
        pw = conf_pw[l].astype(BF16)
        alog, dtb = _lane_vec(dn_a_log[l], SM_A), _lane_vec(dn_dt_bias[l], SM_A)
        ng_t = jnp.tile(dn_norm_g[l], C_HEADS).reshape(1, C_WIDTH)
        last = l == depth - 1
        conf_args = (conf_dw[l], r1(conf_dw_b[l]), r1(conf_ln_g[l]), r1(conf_ln_b[l]), pw, r1(conf_pw_b[l]))
        ffn_args = (r1(grp_norm_a[l]), r1(grp_norm_b[l]), wo, g_ffn, wg, wu, wd, fin)

        qb, kb, vb, _, kf, vf, h, qkvc, z, small = _inproj(xp, g_mix, w_in_p[l], b_in_p[l:l + 1], tm_p)
        logf, ccol, crow = _fox_gates(small, batch, seq)
        oa = _fox_prompt(qb, kb, vb, ccol, crow, batch, seq)
        ob = _conformer(h, *conf_args, batch, seq)
        conf_buf = h.reshape(batch, seq, B_WIDTH)[:, seq - (CONF_WIDTH - 1):]
        dn_buf = qkvc.reshape(batch, seq, 3 * C_WIDTH)[:, seq - (DN_CONV - 1):]
        oc, s_rec = _deltanet(qkvc, z, small, dn_conv[l], alog, dtb, ng_t, batch, seq)
        xp = _out_ffn(xp, oa, ob, oc, *ffn_args, tm_p, last)
        for i, t in enumerate((kf.reshape(batch, seq, A_HEADS, HEAD_DIM), vf.reshape(batch, seq, A_HEADS, HEAD_DIM),
                               logf, conf_buf, dn_buf, s_rec)):
            p_out[i].append(t)

        _, _, _, qf, kf, vf, h, qkvc, z, small = _inproj(xs, g_mix, w_in_p[l], b_in_p[l:l + 1], tm_s)
        conf_new = jnp.concatenate([state_conf_conv[l][:, 1:], h[:, None]], axis=1)
        dn_new = jnp.concatenate([state_dn_conv[l][:, 1:], qkvc[:, None]], axis=1)
        ob, qn, kn, vn, eg, bt, lf = _sample_mix(
            h, state_conf_conv[l], *conf_args, qkvc, state_dn_conv[l], dn_conv[l], small, alog, dtb)
        suffix = _pool_suffix(jnp.swapaxes(cache_logf[l], 1, 2))
        lf_heads = lf[:, SM_F:SM_F + A_HEADS]
        cn = jnp.broadcast_to(lf_heads[:, :, None], (db, A_HEADS, LANES))
        oa = _fox_decode(page_table, qf, kf, vf, cn, cache_k[l].reshape(n_pool, page_rows, A_WIDTH),
                         cache_v[l].reshape(n_pool, page_rows, A_WIDTH), suffix)
        oa = jnp.transpose(oa.reshape(db, A_WIDTH // LANES, LANES), (1, 0, 2))
        oc, s_new = _sample_delta(state_dn_rec[l], qn, kn, vn, eg, bt, z, dn_norm_g[l])
        xs = _out_ffn(xs, oa, ob, oc, *ffn_args, tm_s, last)
        for i, t in enumerate((kf.reshape(db, 1, A_HEADS, HEAD_DIM), vf.reshape(db, 1, A_HEADS, HEAD_DIM),
                               lf_heads.reshape(db, 1, A_HEADS), conf_new, dn_new, s_new)):
            s_out[i].append(t)

    y_prompt = xp.reshape(batch, seq, d)[:, NUM_META:]
    y_sample = xs.reshape(db, 1, d)
    return (y_prompt, y_sample, *[jnp.stack(a) for a in p_out], *[jnp.stack(a) for a in s_out])
```

```python
import functools

import jax
import jax.numpy as jnp
from jax import lax
from jax.experimental import pallas as pl
from jax.experimental.pallas import tpu as pltpu

F32 = jnp.float32
BF16 = jnp.bfloat16
HIGHEST = lax.Precision.HIGHEST

LANES = 128
SUBLANES = 8
VMEM_LIMIT_BYTES = 56 * 1024 * 1024

NUM_META = 16
A_HEADS = 8
HEAD_DIM = 64
A_WIDTH = A_HEADS * HEAD_DIM
A_PAIRS = A_WIDTH // LANES
B_WIDTH = 256
CONF_WIDTH = 31
C_HEADS = 4
C_WIDTH = C_HEADS * HEAD_DIM
DN_CONV = 4
DN_CHUNK = 64
EPS = 1e-6
NEG_BIG = -1e30
LOG2E = 1.4426950408889634

_Q0, _K0, _V0, _GLU0, _QKVC0, _Z0, _SM0 = 0, 512, 1024, 1536, 2048, 2816, 3072
IN_COLS_PAD = 3200
SM_F, SM_A, SM_B = 0, 8, 12


def _cparams(sem):
    return pltpu.CompilerParams(dimension_semantics=sem, vmem_limit_bytes=VMEM_LIMIT_BYTES)


def _const_spec(shape):
    nd = len(shape)
    return pl.BlockSpec(shape, lambda *_: (0,) * nd, pipeline_mode=pl.Buffered(1))


def _dot(a, b, precision=None):
    return jnp.dot(a, b, preferred_element_type=F32, precision=precision)


def _dot_nt(a, b, precision=None):
    return lax.dot_general(a, b, (((1,), (1,)), ((), ())), preferred_element_type=F32, precision=precision)


def _sigmoid(x):
    return 1.0 / (1.0 + jnp.exp(-x))


def _silu(x):
    return x * _sigmoid(x)


def _log_sigmoid(x):
    return jnp.minimum(x, 0.0) - jnp.log1p(jnp.exp(-jnp.abs(x)))


def _softplus(x):
    return jnp.maximum(x, 0.0) + jnp.log1p(jnp.exp(-jnp.abs(x)))


def _rms(x, g):
    return x * lax.rsqrt(jnp.mean(x * x, axis=-1, keepdims=True) + EPS) * g


def _iota(shape, dim):
    return lax.broadcasted_iota(jnp.int32, shape, dim)


def _split2(x):
    hi = x.astype(BF16)
    return hi, (x - hi.astype(F32)).astype(BF16)


def _split3(x):
    hi = x.astype(BF16)
    r = x - hi.astype(F32)
    mid = r.astype(BF16)
    return hi, mid, (r - mid.astype(F32)).astype(BF16)


def _dot_split(a, b):
    (ah, al), (bh, bl) = a, b
    return _dot(ah, bh) + (_dot(ah, bl) + _dot(al, bh))


def _lane_col(x, lane):
    return jnp.sum(jnp.where(_iota((1, x.shape[1]), 1) == lane, x, 0.0), axis=1, keepdims=True)


def _inproj_kernel(x_ref, g_ref, w_ref, b_ref, qb_ref, kb_ref, vb_ref, qf_ref, kf_ref, vf_ref,
                   h_ref, qkvc_ref, z_ref, small_ref):
    xn = _rms(x_ref[...], g_ref[...]).astype(BF16)

    def seg(lo, hi):
        return _dot(xn, w_ref[:, lo:hi]) + b_ref[:, lo:hi]

    q = seg(_Q0, _K0) * (HEAD_DIM ** -0.5 * LOG2E)
    k = seg(_K0, _V0)
    v = seg(_V0, _GLU0)
    qf_ref[...] = q
    kf_ref[...] = k
    vf_ref[...] = v
    for p in range(A_PAIRS):
        sl = slice(p * LANES, (p + 1) * LANES)
        qb_ref[p] = q[:, sl].astype(BF16)
        kb_ref[p] = k[:, sl].astype(BF16)
        vb_ref[p] = v[:, sl].astype(BF16)
    glu = seg(_GLU0, _QKVC0)
    h_ref[...] = glu[:, :B_WIDTH] * _sigmoid(glu[:, B_WIDTH:])
    qkvc_ref[...] = seg(_QKVC0, _Z0)
    z_ref[...] = seg(_Z0, _SM0)
    small_ref[...] = seg(_SM0, IN_COLS_PAD)


def _inproj(x, g, w, b, tm):
    n, d = x.shape
    row = lambda width: pl.BlockSpec((tm, width), lambda i: (i, 0))
    pair = pl.BlockSpec((A_PAIRS, tm, LANES), lambda i: (0, i, 0))
    pair_shape = jax.ShapeDtypeStruct((A_PAIRS, n, LANES), BF16)
    f = lambda width: jax.ShapeDtypeStruct((n, width), F32)
    return pl.pallas_call(
        _inproj_kernel,
        grid=(n // tm,),
        in_specs=[row(d), _const_spec((1, d)), _const_spec(w.shape), _const_spec(b.shape)],
        out_specs=[pair, pair, pair, row(A_WIDTH), row(A_WIDTH), row(A_WIDTH), row(B_WIDTH),
                   row(3 * C_WIDTH), row(C_WIDTH), row(LANES)],
        out_shape=[pair_shape, pair_shape, pair_shape, f(A_WIDTH), f(A_WIDTH), f(A_WIDTH), f(B_WIDTH),
                   f(3 * C_WIDTH), f(C_WIDTH), f(LANES)],
        compiler_params=_cparams(("parallel",)),
        name="inproj",
    )(x, g, w, b)


AUG_PARTS = 3
AUG_HEAD = 2 * AUG_PARTS


def _fox_gates_kernel(small_ref, logf_ref, qaug_ref, kaug_ref, c_ref, *, seq):
    lf = _log_sigmoid(small_ref[...])
    logf_ref[...] = lf[:, SM_F:SM_F + A_HEADS]
    tril = (_iota((LANES, LANES), 0) >= _iota((LANES, LANES), 1)).astype(F32)
    carry = jnp.zeros((1, LANES), F32)
    full = seq // LANES
    for j in range(full):
        cs = _dot(tril, lf[j * LANES:(j + 1) * LANES], HIGHEST) + carry
        c_ref[j * LANES:(j + 1) * LANES, :] = cs
        carry = cs[LANES - 1:LANES]
    tail = seq - full * LANES
    if tail:
        c_ref[full * LANES:seq, :] = _dot(tril[:tail, :tail], lf[full * LANES:seq], HIGHEST) + carry

    parts = _split3(c_ref[...] * LOG2E)
    row, col = _iota((LANES, LANES), 0), _iota((LANES, LANES), 1)
    p3 = None
    for i, part in enumerate(parts):
        t = _dot(part, ((col == AUG_PARTS * row + i) & (row < A_HEADS)).astype(BF16))
        p3 = t if p3 is None else p3 + t
    p3 = p3.astype(BF16)
    lane = _iota((1, LANES), 1)
    slot = lane // AUG_PARTS
    for p in range(A_PAIRS):
        src = AUG_HEAD * p + (col // AUG_HEAD) * AUG_PARTS + col % AUG_PARTS
        hit = row == src
        cslot = col // AUG_PARTS
        gq = (hit & ((cslot == 0) | (cslot == 2))).astype(BF16)
        gk = (hit & ((cslot == 1) | (cslot == 3))).astype(BF16)
        one_q = ((slot == 1) | (slot == 3)).astype(F32)
        one_k = ((slot == 0) | (slot == 2)).astype(F32)
        qaug_ref[p] = (_dot(p3, gq) + one_q).astype(BF16)
        kaug_ref[p] = (one_k - _dot(p3, gk)).astype(BF16)


def _fox_gates(small, batch, seq):
    pair = pl.BlockSpec((A_PAIRS, seq, LANES), lambda b: (0, b, 0))
    pair_shape = jax.ShapeDtypeStruct((A_PAIRS, batch * seq, LANES), BF16)
    return pl.pallas_call(
        functools.partial(_fox_gates_kernel, seq=seq),
        grid=(batch,),
        in_specs=[pl.BlockSpec((seq, LANES), lambda b: (b, 0))],
        out_specs=[pl.BlockSpec((None, seq, A_HEADS), lambda b: (b, 0, 0)), pair, pair],
        out_shape=[jax.ShapeDtypeStruct((batch, seq, A_HEADS), F32), pair_shape, pair_shape],
        scratch_shapes=[pltpu.VMEM((seq, LANES), F32)],
        compiler_params=_cparams(("parallel",)),
        name="fox_gates",
    )(small)


Q_ROWS = 256


def _fox_prompt_kernel(q_ref, k_ref, v_ref, qa_ref, ka_ref, o_ref, kcat_ref, v0_ref, v1_ref, *, seq):
    lane = _iota((1, LANES), 1)
    lo = lane < HEAD_DIM
    zero = jnp.zeros((), BF16)
    kcat_ref[:, 0:LANES] = k_ref[...]
    kcat_ref[:, LANES:2 * LANES] = ka_ref[...]
    v = v_ref[...]
    v0_ref[...] = jnp.where(lo, v, zero)
    v1_ref[...] = jnp.where(lo, zero, v)
    heads = ((lo, lane < AUG_HEAD, v0_ref), (~lo, (lane >= AUG_HEAD) & (lane < 2 * AUG_HEAD), v1_ref))

    def block(r0, nq):
        q, qa = q_ref[r0:r0 + nq, :], qa_ref[r0:r0 + nq, :]
        causal = _iota((nq, nq), 0) >= _iota((nq, nq), 1)
        out = None
        for lanes_h, aug_h, vh_ref in heads:
            qc = jnp.concatenate([jnp.where(lanes_h, q, zero), jnp.where(aug_h, qa, zero)], axis=1)
            s_d = jnp.where(causal, _dot_nt(qc, kcat_ref[r0:r0 + nq, :]), NEG_BIG)
            m = jnp.max(s_d, axis=1, keepdims=True)
            if r0:
                s_f = _dot_nt(qc, kcat_ref[0:r0, :])
                m = jnp.maximum(m, jnp.max(s_f, axis=1, keepdims=True))
            p_d = jnp.exp2(s_d - m)
            l = jnp.sum(p_d, axis=1, keepdims=True)
            o = _dot(p_d.astype(BF16), vh_ref[r0:r0 + nq, :])
            if r0:
                p_f = jnp.exp2(s_f - m)
                l = l + jnp.sum(p_f, axis=1, keepdims=True)
                o = o + _dot(p_f.astype(BF16), vh_ref[0:r0, :])
            o = o * (1.0 / l)
            out = o if out is None else out + o
        o_ref[r0:r0 + nq, :] = out

    full = seq // Q_ROWS
    for i in range(full):
        block(i * Q_ROWS, Q_ROWS)
    if seq - full * Q_ROWS:
        block(full * Q_ROWS, seq - full * Q_ROWS)


def _fox_prompt(qb, kb, vb, qaug, kaug, batch, seq):
    n = batch * seq
    tile = pl.BlockSpec((None, seq, LANES), lambda b, p: (p, b, 0))
    return pl.pallas_call(
        functools.partial(_fox_prompt_kernel, seq=seq),
        grid=(batch, A_PAIRS),
        in_specs=[tile] * 5,
        out_specs=tile,
        out_shape=jax.ShapeDtypeStruct((A_PAIRS, n, LANES), F32),
        scratch_shapes=[pltpu.VMEM((seq, 2 * LANES), BF16), pltpu.VMEM((seq, LANES), BF16),
                        pltpu.VMEM((seq, LANES), BF16)],
        compiler_params=_cparams(("parallel", "parallel")),
        name="fox_prompt",
    )(qb, kb, vb, qaug, kaug)


CONF_ROWS = 48
CONF_PAD = 32


def _conformer_kernel(h_ref, dw_ref, dwb_ref, lng_ref, lnb_ref, pw_ref, pwb_ref, ob_ref, hp_ref, *, seq):
    hp_ref[0:CONF_PAD, :] = jnp.zeros((CONF_PAD, B_WIDTH), F32)
    hp_ref[CONF_PAD:CONF_PAD + seq, :] = h_ref[...]
    shift = CONF_PAD - (CONF_WIDTH - 1)

    def chunk(c, carry):
        r0 = pl.multiple_of(c * CONF_ROWS, SUBLANES)
        win = hp_ref[pl.ds(r0, CONF_ROWS + CONF_PAD), :]
        acc = jnp.zeros((CONF_ROWS, B_WIDTH), F32) + dwb_ref[...]
        for w in range(CONF_WIDTH):
            acc = acc + dw_ref[w:w + 1, :] * win[shift + w:shift + w + CONF_ROWS, :]
        mu = jnp.mean(acc, axis=-1, keepdims=True)
        xc = acc - mu
        var = jnp.mean(xc * xc, axis=-1, keepdims=True)
        y = xc * lax.rsqrt(var + EPS) * lng_ref[...] + lnb_ref[...]
        ob_ref[pl.ds(r0, CONF_ROWS), :] = _dot(_silu(y).astype(BF16), pw_ref[...]) + pwb_ref[...]
        return carry

    lax.fori_loop(0, seq // CONF_ROWS, chunk, 0)


def _conformer(h, dw, dwb, lng, lnb, pw, pwb, batch, seq):
    assert seq % CONF_ROWS == 0
    vec = _const_spec((1, B_WIDTH))
    return pl.pallas_call(
        functools.partial(_conformer_kernel, seq=seq),
        grid=(batch,),
        in_specs=[pl.BlockSpec((seq, B_WIDTH), lambda b: (b, 0)), _const_spec(dw.shape), vec, vec, vec,
                  _const_spec(pw.shape), vec],
        out_specs=pl.BlockSpec((seq, B_WIDTH), lambda b: (b, 0)),
        out_shape=jax.ShapeDtypeStruct((batch * seq, B_WIDTH), F32),
        scratch_shapes=[pltpu.VMEM((CONF_PAD + seq, B_WIDTH), F32)],
        compiler_params=_cparams(("parallel",)),
        name="conformer",
    )(h, dw, dwb, lng, lnb, pw, pwb)


DN_PAD = 8
DN_PREP_ROWS = 344


def _divisor_rows(seq, cap):
    return max(r for r in range(SUBLANES, cap + 1, SUBLANES) if seq % r == 0)


def _head_sum_matrix():
    return ((_iota((C_WIDTH, C_WIDTH), 0) // HEAD_DIM) == (_iota((C_WIDTH, C_WIDTH), 1) // HEAD_DIM)).astype(BF16)


def _head_sumsq(x, ones_bd):
    hi, lo = _split2(x * x)
    return _dot(hi, ones_bd) + _dot(lo, ones_bd)


def _unit_lower_inverse(ns, size):
    n = ns.shape[0]
    eye = (_iota((n, n), 0) == _iota((n, n), 1)).astype(F32)
    inv = eye - ns
    pw = _split2(ns)
    k = 2
    while k < size:
        pw = _split2(_dot_split(pw, pw))
        inv = inv + _dot_split(_split2(inv), pw)
        k *= 2
    return _split2(inv)


def _expand_heads(x):
    head = _iota((1, C_WIDTH), 1) // HEAD_DIM
    return jnp.concatenate([jnp.where(head == h, x, 0.0) for h in range(C_HEADS)], axis=0)


def _delta_chunk(q, k, v, g, bt, s_bd, chunk):
    c = chunk
    n = C_HEADS * c
    tril_c = (_iota((c, c), 0) >= _iota((c, c), 1)).astype(F32)
    gcum = _dot(tril_c, g, HIGHEST)
    gcol = jnp.concatenate([_lane_col(gcum, SM_A + h) for h in range(C_HEADS)], axis=0)
    bcol = jnp.concatenate([_lane_col(bt, SM_B + h) for h in range(C_HEADS)], axis=0)
    glast = jnp.concatenate([jnp.broadcast_to(_lane_col(gcum[c - 1:c], SM_A + h), (c, 1))
                             for h in range(C_HEADS)], axis=0)
    gmask = jnp.concatenate([jnp.where(_iota((1, LANES), 1) == SM_A + h, gcum, 0.0) for h in range(C_HEADS)], axis=0)
    ones_b = jnp.ones((n, LANES), BF16)
    grow = None
    for part in _split3(gmask):
        t = _dot_nt(ones_b, part)
        grow = t if grow is None else grow + t
    ri, ci = _iota((n, n), 0), _iota((n, n), 1)
    same = (ri // c) == (ci // c)
    tri = same & (ri >= ci)
    strict = same & (ri > ci)
    decay = jnp.where(tri, jnp.exp(jnp.where(tri, gcol - grow, 0.0)), 0.0)

    kx, qx, vx = _expand_heads(k), _expand_heads(q), _expand_heads(v)
    kb = kx * bcol
    kxb = kx.astype(BF16)
    a_strict = jnp.where(strict, _dot_nt(kb.astype(BF16), kxb) * decay, 0.0)
    inv = _unit_lower_inverse(a_strict, c)
    eg = jnp.exp(gcol)
    sol = _dot_split(inv, _split2(jnp.concatenate([vx * bcol, kb * eg], axis=1)))
    u0, w = sol[:, :C_WIDTH], sol[:, C_WIDTH:]
    qk = _dot_nt(qx.astype(BF16), kxb) * decay
    q_dec = qx * eg
    k_dec = kx * jnp.exp(glast - gcol)
    sb = s_bd.astype(BF16)
    u = u0 - _dot(w.astype(BF16), sb)
    ub = u.astype(BF16)
    o = _dot(q_dec.astype(BF16), sb) + _dot(qk.astype(BF16), ub)
    gl_rows = jnp.concatenate([jnp.broadcast_to(jnp.exp(_lane_col(gcum[c - 1:c], SM_A + h)), (HEAD_DIM, 1))
                               for h in range(C_HEADS)], axis=0)
    s_new = s_bd * gl_rows + _dot(k_dec.T.astype(BF16), ub)
    o_c = o[0:c]
    for h in range(1, C_HEADS):
        o_c = o_c + o[h * c:(h + 1) * c]
    return o_c, s_new


def _deltanet_kernel(x_ref, z_ref, small_ref, cw_ref, alog_ref, dtb_ref, ng_ref, oc_ref, srec_ref,
                     xp_ref, q_ref, k_ref, v_ref, g_ref, bt_ref, *, seq):
    keep = DN_CONV - 1
    xp_ref[0:DN_PAD, :] = jnp.zeros((DN_PAD, 3 * C_WIDTH), F32)
    xp_ref[DN_PAD:DN_PAD + seq, :] = x_ref[...]
    ones_bd = _head_sum_matrix()
    sm = small_ref[...]
    g_ref[...] = -jnp.exp(alog_ref[...]) * _softplus(sm + dtb_ref[...])
    bt_ref[...] = _sigmoid(sm)

    shift = DN_PAD - keep
    prep = _divisor_rows(seq, DN_PREP_ROWS)
    for c in range(seq // prep):
        r0 = c * prep
        acc = None
        for w in range(DN_CONV):
            t = cw_ref[w:w + 1, :] * xp_ref[r0 + shift + w:r0 + shift + w + prep, :]
            acc = t if acc is None else acc + t
        y = _silu(acc)
        qq, kk, vv = y[:, :C_WIDTH], y[:, C_WIDTH:2 * C_WIDTH], y[:, 2 * C_WIDTH:]
        rows = slice(r0, r0 + prep)
        q_ref[rows, :] = qq * lax.rsqrt(_head_sumsq(qq, ones_bd) + EPS) * (HEAD_DIM ** -0.5)
        k_ref[rows, :] = kk * lax.rsqrt(_head_sumsq(kk, ones_bd) + EPS)
        v_ref[rows, :] = vv

    def run_chunk(r0, chunk, s_bd):
        rows = pl.ds(r0, chunk)
        o, s_bd = _delta_chunk(q_ref[rows, :], k_ref[rows, :], v_ref[rows, :], g_ref[rows, :], bt_ref[rows, :],
                               s_bd, chunk)
        oc_ref[rows, :] = o
        return s_bd

    s_bd = run_chunk(0, NUM_META, jnp.zeros((C_WIDTH, C_WIDTH), F32))

    def body(i, s_bd):
        return run_chunk(pl.multiple_of(NUM_META + i * DN_CHUNK, SUBLANES), DN_CHUNK, s_bd)

    s_bd = lax.fori_loop(0, (seq - NUM_META) // DN_CHUNK, body, s_bd)
    half = s_bd[:, 0:LANES] + s_bd[:, LANES:2 * LANES]
    fold = half + pltpu.roll(half, HEAD_DIM, axis=1)
    srec_ref[...] = fold[:, 0:HEAD_DIM].reshape(C_HEADS, HEAD_DIM, HEAD_DIM)

    for c in range(seq // prep):
        rows = slice(c * prep, (c + 1) * prep)
        o = oc_ref[rows, :]
        ms = _head_sumsq(o, ones_bd) * (1.0 / HEAD_DIM)
        oc_ref[rows, :] = o * lax.rsqrt(ms + EPS) * ng_ref[...] * _silu(z_ref[rows, :])


def _deltanet(qkvc, z, small, cw, alog, dtb, ng, batch, seq):
    vec = _const_spec((1, LANES))
    rows = lambda width: pl.BlockSpec((seq, width), lambda b: (b, 0))
    scr = lambda width: pltpu.VMEM((seq, width), F32)
    return pl.pallas_call(
        functools.partial(_deltanet_kernel, seq=seq),
        grid=(batch,),
        in_specs=[rows(3 * C_WIDTH), rows(C_WIDTH), rows(LANES), _const_spec(cw.shape), vec, vec,
                  _const_spec((1, C_WIDTH))],
        out_specs=[rows(C_WIDTH),
                   pl.BlockSpec((None, C_HEADS, HEAD_DIM, HEAD_DIM), lambda b: (b, 0, 0, 0))],
        out_shape=[jax.ShapeDtypeStruct((batch * seq, C_WIDTH), F32),
                   jax.ShapeDtypeStruct((batch, C_HEADS, HEAD_DIM, HEAD_DIM), F32)],
        scratch_shapes=[pltpu.VMEM((DN_PAD + seq, 3 * C_WIDTH), F32), scr(C_WIDTH), scr(C_WIDTH), scr(C_WIDTH),
                        scr(LANES), scr(LANES)],
        compiler_params=_cparams(("parallel",)),
        name="deltanet",
    )(qkvc, z, small, cw, alog, dtb, ng)


def _out_ffn_kernel(x_ref, oa_ref, ob_ref, oc_ref, ga_ref, gb_ref, wo_ref, gf_ref, wg_ref, wu_ref, wd_ref,
                    fin_ref, y_ref, *, final):
    oa = jnp.concatenate([oa_ref[p] for p in range(A_PAIRS)], axis=-1)
    mixed = jnp.concatenate([_rms(oa, ga_ref[...]), _rms(ob_ref[...], gb_ref[...]), oc_ref[...]], axis=-1)
    x1 = x_ref[...] + _dot(mixed.astype(BF16), wo_ref[...])
    xn = _rms(x1, gf_ref[...]).astype(BF16)
    hid = _silu(_dot(xn, wg_ref[...])) * _dot(xn, wu_ref[...])
    x2 = x1 + _dot(hid.astype(BF16), wd_ref[...])
    y_ref[...] = _rms(x2, fin_ref[...]) if final else x2


def _out_ffn(x, oa, ob, oc, ga, gb, wo, gf, wg, wu, wd, fin, tm, final):
    n, d = x.shape
    row = lambda width: pl.BlockSpec((tm, width), lambda i: (i, 0))
    return pl.pallas_call(
        functools.partial(_out_ffn_kernel, final=final),
        grid=(n // tm,),
        in_specs=[row(d), pl.BlockSpec((A_PAIRS, tm, LANES), lambda i: (0, i, 0)), row(B_WIDTH),
                  row(C_WIDTH), _const_spec(ga.shape), _const_spec(gb.shape), _const_spec(wo.shape),
                  _const_spec(gf.shape), _const_spec(wg.shape), _const_spec(wu.shape), _const_spec(wd.shape),
                  _const_spec(fin.shape)],
        out_specs=row(d),
        out_shape=jax.ShapeDtypeStruct((n, d), F32),
        compiler_params=_cparams(("parallel",)),
        name="out_ffn",
    )(x, oa, ob, oc, ga, gb, wo, gf, wg, wu, wd, fin)


POOL_BLOCK = 64
PAGES_PER_STEP = 8


def _pool_suffix_kernel(lf_ref, out_ref):
    pages, rows, heads = lf_ref.shape
    x = jnp.swapaxes(lf_ref[...], 1, 2).reshape(pages * heads, rows)
    parts = _split3(x)
    later = (_iota((rows, rows), 0) > _iota((rows, rows), 1)).astype(BF16)
    ones = jnp.ones((rows, rows), BF16)
    excl = tot = None
    for part in parts:
        e, t = _dot(part, later), _dot(part, ones)
        excl = e if excl is None else excl + e
        tot = t if tot is None else tot + t
    out_ref[:, 0:heads, :] = excl.reshape(pages, heads, rows)
    out_ref[:, heads:2 * heads, :] = tot.reshape(pages, heads, rows)


def _pool_suffix(cache_logf):
    depth, n_pool, rows, heads = cache_logf.shape
    return pl.pallas_call(
        _pool_suffix_kernel,
        grid=(depth, pl.cdiv(n_pool, POOL_BLOCK)),
        in_specs=[pl.BlockSpec((None, POOL_BLOCK, rows, heads), lambda l, i: (l, i, 0, 0))],
        out_specs=pl.BlockSpec((None, POOL_BLOCK, 2 * heads, rows), lambda l, i: (l, i, 0, 0)),
        out_shape=jax.ShapeDtypeStruct((depth, n_pool, 2 * heads, rows), F32),
        compiler_params=_cparams(("parallel", "parallel")),
        name="pool_suffix",
    )(cache_logf)


def _fox_decode_kernel(pt_ref, q_ref, kn_ref, vn_ref, cn_ref, *rest):
    pps = PAGES_PER_STEP
    k_refs, v_refs, r_refs = rest[0:pps], rest[pps:2 * pps], rest[2 * pps:3 * pps]
    o_ref, m_ref, l_ref, acc_ref, suf_ref = rest[3 * pps:]
    j = pl.program_id(1)
    q = q_ref[...]

    @pl.when(j == 0)
    def _():
        m_ref[...] = jnp.broadcast_to(jnp.sum(q * kn_ref[...], axis=1, keepdims=True), m_ref.shape)
        l_ref[...] = jnp.ones(l_ref.shape, F32)
        acc_ref[...] = jnp.concatenate([vn_ref[...], vn_ref[...]], axis=1)
        suf_ref[...] = jnp.zeros(suf_ref.shape, F32)

    row = _iota((A_HEADS, LANES), 0)
    lo = _iota((A_HEADS, LANES), 1) < HEAD_DIM
    q2 = jnp.concatenate([q, q], axis=1)
    pair_rows = [(row == 2 * i) | (row == 2 * i + 1) for i in range(A_PAIRS)]
    own_half = (row % 2 == 0) == lo
    qm = [jnp.where(pair_rows[i] & own_half, q2, 0.0).astype(BF16) for i in range(A_PAIRS)]
    cn = cn_ref[...]
    m, l, acc, suf = m_ref[...], l_ref[...], acc_ref[...], suf_ref[...]
    rows = k_refs[0].shape[0] // A_HEADS

    def pair_page(ref, i):
        a = ref[pl.ds(2 * i, rows, stride=A_HEADS), :]
        b = ref[pl.ds(2 * i + 1, rows, stride=A_HEADS), :]
        return jnp.concatenate([a, b], axis=1).astype(BF16)

    scores = []
    for u in range(pps):
        r = r_refs[u][...]
        s = (cn + suf + r[0:A_HEADS]) * LOG2E
        for i in range(A_PAIRS):
            s = s + _dot_nt(qm[i], pair_page(k_refs[u], i))
        suf = suf + r[A_HEADS:2 * A_HEADS]
        scores.append(s)
    smax = scores[0]
    for s in scores[1:]:
        smax = jnp.maximum(smax, s)
    m_new = jnp.maximum(m, jnp.max(smax, axis=1, keepdims=True))
    a = jnp.exp2(m - m_new)
    probs = [jnp.exp2(s - m_new) for s in scores]
    psum = probs[0]
    for p in probs[1:]:
        psum = psum + p
    l = a * l + jnp.sum(psum, axis=1, keepdims=True)
    acc = acc * a
    for u in range(pps):
        for i in range(A_PAIRS):
            acc = acc + _dot(jnp.where(pair_rows[i], probs[u], 0.0).astype(BF16), pair_page(v_refs[u], i))
    m_ref[...], l_ref[...], acc_ref[...], suf_ref[...] = m_new, l, acc, suf

    @pl.when(j == pl.num_programs(1) - 1)
    def _():
        o = acc / l
        even = _iota((A_HEADS, HEAD_DIM), 0) % 2 == 0
        o_ref[...] = jnp.where(even, o[:, 0:HEAD_DIM], o[:, HEAD_DIM:LANES])


def _fox_decode(page_table, layer, q, k_new, v_new, logf_new, cache_k, cache_v, suffix):
    db, n_pages = page_table.shape
    pps = PAGES_PER_STEP
    assert n_pages % pps == 0
    steps = n_pages // pps
    depth, n_pool, rows = cache_k.shape[:3]
    cache_k = cache_k.reshape(depth, n_pool, rows * A_HEADS, HEAD_DIM)
    cache_v = cache_v.reshape(depth, n_pool, rows * A_HEADS, HEAD_DIM)
    tok = pl.BlockSpec((None, A_HEADS, HEAD_DIM), lambda b, j, pt: (b, 0, 0))

    def page(u, shape):
        zeros = (0,) * len(shape)
        return pl.BlockSpec((None, None) + shape,
                            lambda b, j, pt: (layer, pt[b, n_pages - 1 - (j * pps + u)]) + zeros)

    grid_spec = pltpu.PrefetchScalarGridSpec(
        num_scalar_prefetch=1,
        grid=(db, steps),
        in_specs=[tok, tok, tok, pl.BlockSpec((None, A_HEADS, LANES), lambda b, j, pt: (b, 0, 0))]
                 + [page(u, (rows * A_HEADS, HEAD_DIM)) for u in range(pps)]
                 + [page(u, (rows * A_HEADS, HEAD_DIM)) for u in range(pps)]
                 + [page(u, (2 * A_HEADS, rows)) for u in range(pps)],
        out_specs=tok,
        scratch_shapes=[pltpu.VMEM((A_HEADS, LANES), F32), pltpu.VMEM((A_HEADS, LANES), F32),
                        pltpu.VMEM((A_HEADS, LANES), F32), pltpu.VMEM((A_HEADS, LANES), F32)],
    )
    tok3 = lambda t: t.reshape(db, A_HEADS, HEAD_DIM)
    return pl.pallas_call(
        _fox_decode_kernel,
        grid_spec=grid_spec,
        out_shape=jax.ShapeDtypeStruct((db, A_HEADS, HEAD_DIM), F32),
        compiler_params=_cparams(("parallel", "arbitrary")),
        name="fox_decode",
    )(page_table, tok3(q), tok3(k_new), tok3(v_new), logf_new, *([cache_k] * pps), *([cache_v] * pps),
      *([suffix] * pps))


def _sample_mix_kernel(h_ref, cbuf_ref, dw_ref, dwb_ref, lng_ref, lnb_ref, pw_ref, pwb_ref,
                       x_ref, dbuf_ref, cw_ref, small_ref, alog_ref, dtb_ref,
                       ob_ref, q_ref, k_ref, v_ref, eg_ref, bt_ref, lf_ref):
    keep = CONF_WIDTH - 1
    h = h_ref[...]
    cbuf = cbuf_ref[...]
    acc = jnp.sum(cbuf * dw_ref[0:keep, :][None], axis=1) + dw_ref[keep:keep + 1, :] * h + dwb_ref[...]
    mu = jnp.mean(acc, axis=-1, keepdims=True)
    xc = acc - mu
    var = jnp.mean(xc * xc, axis=-1, keepdims=True)
    y = xc * lax.rsqrt(var + EPS) * lng_ref[...] + lnb_ref[...]
    ob_ref[...] = _dot(_silu(y).astype(BF16), pw_ref[...]) + pwb_ref[...]

    dk = DN_CONV - 1
    x = x_ref[...]
    dbuf = dbuf_ref[...]
    conv = jnp.sum(dbuf * cw_ref[0:dk, :][None], axis=1) + cw_ref[dk:dk + 1, :] * x
    yv = _silu(conv)
    qq, kk, vv = yv[:, :C_WIDTH], yv[:, C_WIDTH:2 * C_WIDTH], yv[:, 2 * C_WIDTH:]
    ones_bd = _head_sum_matrix()
    q_ref[...] = qq * lax.rsqrt(_head_sumsq(qq, ones_bd) + EPS) * (HEAD_DIM ** -0.5)
    k_ref[...] = kk * lax.rsqrt(_head_sumsq(kk, ones_bd) + EPS)
    v_ref[...] = vv
    sm = small_ref[...]
    eg_ref[...] = jnp.exp(-jnp.exp(alog_ref[...]) * _softplus(sm + dtb_ref[...]))
    bt_ref[...] = _sigmoid(sm)
    lf_ref[...] = _log_sigmoid(sm)


def _sample_mix(h, cbuf, dw, dwb, lng, lnb, pw, pwb, x, dbuf, cw, small, alog, dtb):
    db = h.shape[0]
    f = lambda *shape: jax.ShapeDtypeStruct(shape, F32)
    return pl.pallas_call(
        _sample_mix_kernel,
        out_shape=[f(db, B_WIDTH), f(db, C_WIDTH), f(db, C_WIDTH), f(db, C_WIDTH),
                   f(db, LANES), f(db, LANES), f(db, LANES)],
        compiler_params=pltpu.CompilerParams(vmem_limit_bytes=VMEM_LIMIT_BYTES),
        name="sample_mix",
    )(h, cbuf, dw, dwb, lng, lnb, pw, pwb, x, dbuf, cw, small, alog, dtb)


def _sample_delta_kernel(s_ref, qc_ref, kc_ref, qr_ref, kr_ref, v_ref, eg_ref, bt_ref, z_ref, ng_ref,
                         o_ref, snew_ref):
    sd = s_ref[...] * eg_ref[...]
    ks = jnp.sum(kc_ref[...] * sd, axis=1, keepdims=True)
    u = bt_ref[...] * (v_ref[...] - ks)
    qs = jnp.sum(qc_ref[...] * sd, axis=1, keepdims=True)
    qk = jnp.sum(qr_ref[...] * kr_ref[...], axis=2, keepdims=True)
    o = qs + qk * u
    snew_ref[...] = sd + kc_ref[...] * u
    ms = jnp.mean(o * o, axis=-1, keepdims=True)
    o_ref[...] = o * lax.rsqrt(ms + EPS) * ng_ref[...] * _silu(z_ref[...])


def _sample_delta(s, qn, kn, vn, eg, bt, z, ng):
    db = s.shape[0]
    g = db * C_HEADS
    col = lambda t: t.reshape(g, HEAD_DIM, 1)
    row = lambda t: t.reshape(g, 1, HEAD_DIM)
    gate = lambda t, off: t[:, off:off + C_HEADS].reshape(g, 1, 1)
    o, s_new = pl.pallas_call(
        _sample_delta_kernel,
        out_shape=[jax.ShapeDtypeStruct((g, 1, HEAD_DIM), F32), jax.ShapeDtypeStruct((g, HEAD_DIM, HEAD_DIM), F32)],
        compiler_params=pltpu.CompilerParams(vmem_limit_bytes=VMEM_LIMIT_BYTES),
        name="sample_delta",
    )(s.reshape(g, HEAD_DIM, HEAD_DIM), col(qn), col(kn), row(qn), row(kn), row(vn), gate(eg, SM_A), gate(bt, SM_B),
      row(z), ng.reshape(1, 1, HEAD_DIM))
    return o.reshape(db, C_WIDTH), s_new.reshape(db, C_HEADS, HEAD_DIM, HEAD_DIM)


def _permute_in_cols(t):
    aw, bw, cw = A_WIDTH, B_WIDTH, C_WIDTH
    f0 = 3 * aw
    glu0 = f0 + A_HEADS
    qkv0 = glu0 + 2 * bw
    a0 = qkv0 + 3 * cw
    b0 = a0 + C_HEADS
    z0 = b0 + C_HEADS
    parts = [t[..., 0:f0], t[..., glu0:qkv0], t[..., qkv0:a0], t[..., z0:z0 + cw],
             t[..., f0:glu0], t[..., a0:b0], t[..., b0:z0]]
    out = jnp.concatenate(parts, axis=-1)
    return jnp.pad(out, [(0, 0)] * (t.ndim - 1) + [(0, IN_COLS_PAD - out.shape[-1])])


def _lane_vec(vals, off):
    return jnp.zeros((1, LANES), F32).at[0, off:off + vals.shape[0]].set(vals)


def _row_tile(n):
    for tm in (384, 256, 128, 64, 32, 16, 8):
        if n % tm == 0:
            return tm
    raise ValueError(n)


def kernel(x_prompt, x_sample, cache_k, cache_v, cache_logf, state_conf_conv, state_dn_conv, state_dn_rec, page_table, meta_tokens, norm_mix, w_in, b_in, conf_dw, conf_dw_b, conf_ln_g, conf_ln_b, conf_pw, conf_pw_b, dn_conv, dn_a_log, dn_dt_bias, dn_norm_g, grp_norm_a, grp_norm_b, w_out, norm_ffn, w_ffn_gate, w_ffn_up, w_ffn_down, final_norm):
    batch, seq_in, d = x_prompt.shape
    db = x_sample.shape[0]
    assert x_sample.shape[1] == 1
    depth = w_in.shape[0]
    seq = NUM_META + seq_in
    n = batch * seq

    xp = jnp.concatenate([jnp.broadcast_to(meta_tokens[None], (batch, NUM_META, d)), x_prompt], axis=1).reshape(n, d)
    xs = x_sample.reshape(db, d)
    w_in_p = _permute_in_cols(w_in).astype(BF16)
    b_in_p = _permute_in_cols(b_in)
    tm_p, tm_s = _row_tile(n), _row_tile(db)
    fin = final_norm.reshape(1, d)
    r1 = lambda t: t.reshape(1, -1)
    suffix = _pool_suffix(cache_logf)

    p_out = [[] for _ in range(6)]
    s_out = [[] for _ in range(6)]
    for l in range(depth):
        g_mix, g_ffn = r1(norm_mix[l]), r1(norm_ffn[l])
        wo, wg, wu, wd = (w_out[l].astype(BF16), w_ffn_gate[l].astype(BF16), w_ffn_up[l].astype(BF16),
                          w_ffn_down[l].astype(BF16))
        pw = conf_pw[l].astype(BF16)
        alog, dtb = _lane_vec(dn_a_log[l], SM_A), _lane_vec(dn_dt_bias[l], SM_A)
        ng_t = jnp.tile(dn_norm_g[l], C_HEADS).reshape(1, C_WIDTH)
        last = l == depth - 1
        conf_args = (conf_dw[l], r1(conf_dw_b[l]), r1(conf_ln_g[l]), r1(conf_ln_b[l]), pw, r1(conf_pw_b[l]))
        ffn_args = (r1(grp_norm_a[l]), r1(grp_norm_b[l]), wo, g_ffn, wg, wu, wd, fin)

        qb, kb, vb, _, kf, vf, h, qkvc, z, small = _inproj(xp, g_mix, w_in_p[l], b_in_p[l:l + 1], tm_p)
        logf, qaug, kaug = _fox_gates(small, batch, seq)
        oa = _fox_prompt(qb, kb, vb, qaug, kaug, batch, seq)
        ob = _conformer(h, *conf_args, batch, seq)
        conf_buf = h.reshape(batch, seq, B_WIDTH)[:, seq - (CONF_WIDTH - 1):]
        dn_buf = qkvc.reshape(batch, seq, 3 * C_WIDTH)[:, seq - (DN_CONV - 1):]
        oc, s_rec = _deltanet(qkvc, z, small, dn_conv[l], alog, dtb, ng_t, batch, seq)
        xp = _out_ffn(xp, oa, ob, oc, *ffn_args, tm_p, last)
        for i, t in enumerate((kf.reshape(batch, seq, A_HEADS, HEAD_DIM), vf.reshape(batch, seq, A_HEADS, HEAD_DIM),
                               logf, conf_buf, dn_buf, s_rec)):
            p_out[i].append(t)

        _, _, _, qf, kf, vf, h, qkvc, z, small = _inproj(xs, g_mix, w_in_p[l], b_in_p[l:l + 1], tm_s)
        conf_new = jnp.concatenate([state_conf_conv[l][:, 1:], h[:, None]], axis=1)
        dn_new = jnp.concatenate([state_dn_conv[l][:, 1:], qkvc[:, None]], axis=1)
        ob, qn, kn, vn, eg, bt, lf = _sample_mix(
            h, state_conf_conv[l], *conf_args, qkvc, state_dn_conv[l], dn_conv[l], small, alog, dtb)
        lf_heads = lf[:, SM_F:SM_F + A_HEADS]
        cn = jnp.broadcast_to(lf_heads[:, :, None], (db, A_HEADS, LANES))
        oa = _fox_decode(page_table, l, qf, kf, vf, cn, cache_k, cache_v, suffix)
        oa = jnp.transpose(oa.reshape(db, A_PAIRS, LANES), (1, 0, 2))
        oc, s_new = _sample_delta(state_dn_rec[l], qn, kn, vn, eg, bt, z, dn_norm_g[l])
        xs = _out_ffn(xs, oa, ob, oc, *ffn_args, tm_s, last)
        for i, t in enumerate((kf.reshape(db, 1, A_HEADS, HEAD_DIM), vf.reshape(db, 1, A_HEADS, HEAD_DIM),
                               lf_heads.reshape(db, 1, A_HEADS), conf_new, dn_new, s_new)):
            s_out[i].append(t)

    y_prompt = xp.reshape(batch, seq, d)[:, NUM_META:]
    y_sample = xs.reshape(db, 1, d)
    return (y_prompt, y_sample, *[jnp.stack(a) for a in p_out], *[jnp.stack(a) for a in s_out])
```

```python
import functools

import jax
import jax.numpy as jnp
from jax import lax
from jax.experimental import pallas as pl
from jax.experimental.pallas import tpu as pltpu

F32 = jnp.float32
BF16 = jnp.bfloat16
HIGHEST = lax.Precision.HIGHEST

LANES = 128
SUBLANES = 8
VMEM_LIMIT_BYTES = 56 * 1024 * 1024

NUM_META = 16
A_HEADS = 8
HEAD_DIM = 64
A_WIDTH = A_HEADS * HEAD_DIM
A_PAIRS = A_WIDTH // LANES
B_WIDTH = 256
CONF_WIDTH = 31
C_HEADS = 4
C_WIDTH = C_HEADS * HEAD_DIM
DN_CONV = 4
DN_CHUNK = 64
EPS = 1e-6
NEG_BIG = -1e30
LOG2E = 1.4426950408889634

_Q0, _K0, _V0, _GLU0, _QKVC0, _Z0, _SM0 = 0, 512, 1024, 1536, 2048, 2816, 3072
IN_COLS_PAD = 3200
SM_F, SM_A, SM_B = 0, 8, 12


def _cparams(sem):
    return pltpu.CompilerParams(dimension_semantics=sem, vmem_limit_bytes=VMEM_LIMIT_BYTES)


def _const_spec(shape):
    nd = len(shape)
    return pl.BlockSpec(shape, lambda *_: (0,) * nd, pipeline_mode=pl.Buffered(1))


def _batched(a, b):
    if a.ndim == 2:
        a = jnp.broadcast_to(a, b.shape[:1] + a.shape)
    if b.ndim == 2:
        b = jnp.broadcast_to(b, a.shape[:1] + b.shape)
    return a, b


def _dot(a, b, precision=None):
    if a.ndim == 3 or b.ndim == 3:
        a, b = _batched(a, b)
        return lax.dot_general(a, b, (((2,), (1,)), ((0,), (0,))), preferred_element_type=F32, precision=precision)
    return jnp.dot(a, b, preferred_element_type=F32, precision=precision)


def _dot_nt(a, b, precision=None):
    if a.ndim == 3 or b.ndim == 3:
        a, b = _batched(a, b)
        return lax.dot_general(a, b, (((2,), (2,)), ((0,), (0,))), preferred_element_type=F32, precision=precision)
    return lax.dot_general(a, b, (((1,), (1,)), ((), ())), preferred_element_type=F32, precision=precision)


def _sigmoid(x):
    return 1.0 / (1.0 + jnp.exp(-x))


def _silu(x):
    return x * _sigmoid(x)


def _log_sigmoid(x):
    return jnp.minimum(x, 0.0) - jnp.log1p(jnp.exp(-jnp.abs(x)))


def _softplus(x):
    return jnp.maximum(x, 0.0) + jnp.log1p(jnp.exp(-jnp.abs(x)))


def _rms(x, g):
    return x * lax.rsqrt(jnp.mean(x * x, axis=-1, keepdims=True) + EPS) * g


def _iota(shape, dim):
    return lax.broadcasted_iota(jnp.int32, shape, dim)


def _split2(x):
    hi = x.astype(BF16)
    return hi, (x - hi.astype(F32)).astype(BF16)


def _split3(x):
    hi = x.astype(BF16)
    r = x - hi.astype(F32)
    mid = r.astype(BF16)
    return hi, mid, (r - mid.astype(F32)).astype(BF16)


def _dot_split(a, b):
    (ah, al), (bh, bl) = a, b
    return _dot(ah, bh) + (_dot(ah, bl) + _dot(al, bh))


def _lane_col(x, lane):
    return jnp.sum(jnp.where(_iota((1, x.shape[-1]), 1) == lane, x, 0.0), axis=-1, keepdims=True)


def _inproj_kernel(x_ref, g_ref, w_ref, b_ref, qb_ref, kb_ref, vb_ref, qf_ref, kf_ref, vf_ref,
                   h_ref, qkvc_ref, z_ref, small_ref):
    xn = _rms(x_ref[...], g_ref[...]).astype(BF16)

    def seg(lo, hi):
        return _dot(xn, w_ref[:, lo:hi]) + b_ref[:, lo:hi]

    q = seg(_Q0, _K0) * (HEAD_DIM ** -0.5 * LOG2E)
    k = seg(_K0, _V0)
    v = seg(_V0, _GLU0)
    qf_ref[...] = q
    kf_ref[...] = k
    vf_ref[...] = v
    for p in range(A_PAIRS):
        sl = slice(p * LANES, (p + 1) * LANES)
        qb_ref[p] = q[:, sl].astype(BF16)
        kb_ref[p] = k[:, sl].astype(BF16)
        vb_ref[p] = v[:, sl].astype(BF16)
    glu = seg(_GLU0, _QKVC0)
    h_ref[...] = glu[:, :B_WIDTH] * _sigmoid(glu[:, B_WIDTH:])
    qkvc_ref[...] = seg(_QKVC0, _Z0)
    z_ref[...] = seg(_Z0, _SM0)
    small_ref[...] = seg(_SM0, IN_COLS_PAD)


def _inproj(x, g, w, b, tm):
    n, d = x.shape
    row = lambda width: pl.BlockSpec((tm, width), lambda i: (i, 0))
    pair = pl.BlockSpec((A_PAIRS, tm, LANES), lambda i: (0, i, 0))
    pair_shape = jax.ShapeDtypeStruct((A_PAIRS, n, LANES), BF16)
    f = lambda width: jax.ShapeDtypeStruct((n, width), F32)
    return pl.pallas_call(
        _inproj_kernel,
        grid=(n // tm,),
        in_specs=[row(d), _const_spec((1, d)), _const_spec(w.shape), _const_spec(b.shape)],
        out_specs=[pair, pair, pair, row(A_WIDTH), row(A_WIDTH), row(A_WIDTH), row(B_WIDTH),
                   row(3 * C_WIDTH), row(C_WIDTH), row(LANES)],
        out_shape=[pair_shape, pair_shape, pair_shape, f(A_WIDTH), f(A_WIDTH), f(A_WIDTH), f(B_WIDTH),
                   f(3 * C_WIDTH), f(C_WIDTH), f(LANES)],
        compiler_params=_cparams(("parallel",)),
        name="inproj",
    )(x, g, w, b)


AUG_PARTS = 3
AUG_HEAD = 2 * AUG_PARTS


def _fox_gates_kernel(small_ref, logf_ref, qaug_ref, kaug_ref, c_ref, *, seq):
    lf = _log_sigmoid(small_ref[...])
    logf_ref[...] = lf[:, SM_F:SM_F + A_HEADS]
    tril = (_iota((LANES, LANES), 0) >= _iota((LANES, LANES), 1)).astype(F32)
    carry = jnp.zeros((1, LANES), F32)
    full = seq // LANES
    for j in range(full):
        cs = _dot(tril, lf[j * LANES:(j + 1) * LANES], HIGHEST) + carry
        c_ref[j * LANES:(j + 1) * LANES, :] = cs
        carry = cs[LANES - 1:LANES]
    tail = seq - full * LANES
    if tail:
        c_ref[full * LANES:seq, :] = _dot(tril[:tail, :tail], lf[full * LANES:seq], HIGHEST) + carry

    parts = _split3(c_ref[...] * LOG2E)
    row, col = _iota((LANES, LANES), 0), _iota((LANES, LANES), 1)
    p3 = None
    for i, part in enumerate(parts):
        t = _dot(part, ((col == AUG_PARTS * row + i) & (row < A_HEADS)).astype(BF16))
        p3 = t if p3 is None else p3 + t
    p3 = p3.astype(BF16)
    lane = _iota((1, LANES), 1)
    slot = lane // AUG_PARTS
    for p in range(A_PAIRS):
        src = AUG_HEAD * p + (col // AUG_HEAD) * AUG_PARTS + col % AUG_PARTS
        hit = row == src
        cslot = col // AUG_PARTS
        gq = (hit & ((cslot == 0) | (cslot == 2))).astype(BF16)
        gk = (hit & ((cslot == 1) | (cslot == 3))).astype(BF16)
        one_q = ((slot == 1) | (slot == 3)).astype(F32)
        one_k = ((slot == 0) | (slot == 2)).astype(F32)
        qaug_ref[p] = (_dot(p3, gq) + one_q).astype(BF16)
        kaug_ref[p] = (one_k - _dot(p3, gk)).astype(BF16)


def _fox_gates(small, batch, seq):
    pair = pl.BlockSpec((A_PAIRS, seq, LANES), lambda b: (0, b, 0))
    pair_shape = jax.ShapeDtypeStruct((A_PAIRS, batch * seq, LANES), BF16)
    return pl.pallas_call(
        functools.partial(_fox_gates_kernel, seq=seq),
        grid=(batch,),
        in_specs=[pl.BlockSpec((seq, LANES), lambda b: (b, 0))],
        out_specs=[pl.BlockSpec((None, seq, A_HEADS), lambda b: (b, 0, 0)), pair, pair],
        out_shape=[jax.ShapeDtypeStruct((batch, seq, A_HEADS), F32), pair_shape, pair_shape],
        scratch_shapes=[pltpu.VMEM((seq, LANES), F32)],
        compiler_params=_cparams(("parallel",)),
        name="fox_gates",
    )(small)


Q_ROWS = 256


def _fox_prompt_kernel(q_ref, k_ref, v_ref, qa_ref, ka_ref, o_ref, kcat_ref, v0_ref, v1_ref, *, seq):
    lane = _iota((1, LANES), 1)
    lo = lane < HEAD_DIM
    zero = jnp.zeros((), BF16)
    kcat_ref[:, 0:LANES] = k_ref[...]
    kcat_ref[:, LANES:2 * LANES] = ka_ref[...]
    v = v_ref[...]
    v0_ref[...] = jnp.where(lo, v, zero)
    v1_ref[...] = jnp.where(lo, zero, v)
    heads = ((lo, lane < AUG_HEAD, v0_ref), (~lo, (lane >= AUG_HEAD) & (lane < 2 * AUG_HEAD), v1_ref))

    def block(r0, nq):
        q, qa = q_ref[r0:r0 + nq, :], qa_ref[r0:r0 + nq, :]
        causal = _iota((nq, nq), 0) >= _iota((nq, nq), 1)
        out = None
        for lanes_h, aug_h, vh_ref in heads:
            qc = jnp.concatenate([jnp.where(lanes_h, q, zero), jnp.where(aug_h, qa, zero)], axis=1)
            s_d = jnp.where(causal, _dot_nt(qc, kcat_ref[r0:r0 + nq, :]), NEG_BIG)
            m = jnp.max(s_d, axis=1, keepdims=True)
            if r0:
                s_f = _dot_nt(qc, kcat_ref[0:r0, :])
                m = jnp.maximum(m, jnp.max(s_f, axis=1, keepdims=True))
            p_d = jnp.exp2(s_d - m)
            l = jnp.sum(p_d, axis=1, keepdims=True)
            o = _dot(p_d.astype(BF16), vh_ref[r0:r0 + nq, :])
            if r0:
                p_f = jnp.exp2(s_f - m)
                l = l + jnp.sum(p_f, axis=1, keepdims=True)
                o = o + _dot(p_f.astype(BF16), vh_ref[0:r0, :])
            o = o * (1.0 / l)
            out = o if out is None else out + o
        o_ref[r0:r0 + nq, :] = out

    full = seq // Q_ROWS
    for i in range(full):
        block(i * Q_ROWS, Q_ROWS)
    if seq - full * Q_ROWS:
        block(full * Q_ROWS, seq - full * Q_ROWS)


def _fox_prompt(qb, kb, vb, qaug, kaug, batch, seq):
    n = batch * seq
    tile = pl.BlockSpec((None, seq, LANES), lambda b, p: (p, b, 0))
    return pl.pallas_call(
        functools.partial(_fox_prompt_kernel, seq=seq),
        grid=(batch, A_PAIRS),
        in_specs=[tile] * 5,
        out_specs=tile,
        out_shape=jax.ShapeDtypeStruct((A_PAIRS, n, LANES), F32),
        scratch_shapes=[pltpu.VMEM((seq, 2 * LANES), BF16), pltpu.VMEM((seq, LANES), BF16),
                        pltpu.VMEM((seq, LANES), BF16)],
        compiler_params=_cparams(("parallel", "parallel")),
        name="fox_prompt",
    )(qb, kb, vb, qaug, kaug)


CONF_ROWS = 48
CONF_PAD = 32


def _conformer_kernel(h_ref, dw_ref, dwb_ref, lng_ref, lnb_ref, pw_ref, pwb_ref, ob_ref, hp_ref, *, seq):
    hp_ref[0:CONF_PAD, :] = jnp.zeros((CONF_PAD, B_WIDTH), F32)
    hp_ref[CONF_PAD:CONF_PAD + seq, :] = h_ref[...]
    shift = CONF_PAD - (CONF_WIDTH - 1)

    def chunk(c, carry):
        r0 = pl.multiple_of(c * CONF_ROWS, SUBLANES)
        win = hp_ref[pl.ds(r0, CONF_ROWS + CONF_PAD), :]
        acc = jnp.zeros((CONF_ROWS, B_WIDTH), F32) + dwb_ref[...]
        for w in range(CONF_WIDTH):
            acc = acc + dw_ref[w:w + 1, :] * win[shift + w:shift + w + CONF_ROWS, :]
        mu = jnp.mean(acc, axis=-1, keepdims=True)
        xc = acc - mu
        var = jnp.mean(xc * xc, axis=-1, keepdims=True)
        y = xc * lax.rsqrt(var + EPS) * lng_ref[...] + lnb_ref[...]
        ob_ref[pl.ds(r0, CONF_ROWS), :] = _dot(_silu(y).astype(BF16), pw_ref[...]) + pwb_ref[...]
        return carry

    lax.fori_loop(0, seq // CONF_ROWS, chunk, 0)


def _conformer(h, dw, dwb, lng, lnb, pw, pwb, batch, seq):
    assert seq % CONF_ROWS == 0
    vec = _const_spec((1, B_WIDTH))
    return pl.pallas_call(
        functools.partial(_conformer_kernel, seq=seq),
        grid=(batch,),
        in_specs=[pl.BlockSpec((seq, B_WIDTH), lambda b: (b, 0)), _const_spec(dw.shape), vec, vec, vec,
                  _const_spec(pw.shape), vec],
        out_specs=pl.BlockSpec((seq, B_WIDTH), lambda b: (b, 0)),
        out_shape=jax.ShapeDtypeStruct((batch * seq, B_WIDTH), F32),
        scratch_shapes=[pltpu.VMEM((CONF_PAD + seq, B_WIDTH), F32)],
        compiler_params=_cparams(("parallel",)),
        name="conformer",
    )(h, dw, dwb, lng, lnb, pw, pwb)


DN_PAD = 8
DN_PREP_ROWS = 344


def _divisor_rows(seq, cap):
    return max(r for r in range(SUBLANES, cap + 1, SUBLANES) if seq % r == 0)


def _head_sum_matrix():
    return ((_iota((C_WIDTH, C_WIDTH), 0) // HEAD_DIM) == (_iota((C_WIDTH, C_WIDTH), 1) // HEAD_DIM)).astype(BF16)


def _head_sumsq(x, ones_bd):
    hi, lo = _split2(x * x)
    return _dot(hi, ones_bd) + _dot(lo, ones_bd)


def _unit_lower_inverse(ns, size):
    n = ns.shape[-1]
    eye = (_iota((n, n), 0) == _iota((n, n), 1)).astype(F32)
    inv = eye - ns
    pw = ns.astype(BF16)
    k = 2
    while k < size:
        pw = _dot(pw, pw).astype(BF16)
        inv = inv + _dot(inv.astype(BF16), pw)
        k *= 2
    resid = (eye - inv) - _dot_split(_split2(ns), _split2(inv))
    return _split2(inv + _dot(inv.astype(BF16), resid.astype(BF16)))


def _expand_heads(x):
    head = _iota((1, C_WIDTH), 1) // HEAD_DIM
    return jnp.concatenate([jnp.where(head == h, x, 0.0) for h in range(C_HEADS)], axis=-2)


def _delta_chunk(q, k, v, g, bt, s_bd, chunk):
    c = chunk
    n = C_HEADS * c
    lead = q.shape[:-2]
    tril_c = (_iota((c, c), 0) >= _iota((c, c), 1)).astype(F32)
    gcum = _dot(tril_c, g, HIGHEST)
    last = gcum[..., c - 1:c, :]
    rows_cat = lambda parts: jnp.concatenate(parts, axis=-2)
    gcol = rows_cat([_lane_col(gcum, SM_A + h) for h in range(C_HEADS)])
    bcol = rows_cat([_lane_col(bt, SM_B + h) for h in range(C_HEADS)])
    glast = rows_cat([jnp.broadcast_to(_lane_col(last, SM_A + h), lead + (c, 1)) for h in range(C_HEADS)])
    gmask = rows_cat([jnp.where(_iota((1, LANES), 1) == SM_A + h, gcum, 0.0) for h in range(C_HEADS)])
    ones_b = jnp.ones((n, LANES), BF16)
    grow = None
    for part in _split3(gmask):
        t = _dot_nt(ones_b, part)
        grow = t if grow is None else grow + t
    ri, ci = _iota((n, n), 0), _iota((n, n), 1)
    same = (ri // c) == (ci // c)
    tri = same & (ri >= ci)
    strict = same & (ri > ci)
    decay = jnp.where(tri, jnp.exp(jnp.where(tri, gcol - grow, 0.0)), 0.0)

    kx, qx, vx = _expand_heads(k), _expand_heads(q), _expand_heads(v)
    kb = kx * bcol
    kxb = kx.astype(BF16)
    a_strict = jnp.where(strict, _dot_nt(kb.astype(BF16), kxb) * decay, 0.0)
    inv = _unit_lower_inverse(a_strict, c)
    eg = jnp.exp(gcol)
    sol = _dot_split(inv, _split2(jnp.concatenate([vx * bcol, kb * eg], axis=-1)))
    u0, w = sol[..., :C_WIDTH], sol[..., C_WIDTH:]
    qk = _dot_nt(qx.astype(BF16), kxb) * decay
    q_dec = qx * eg
    k_dec = kx * jnp.exp(glast - gcol)
    sb = s_bd.astype(BF16)
    u = u0 - _dot(w.astype(BF16), sb)
    ub = u.astype(BF16)
    o = _dot(q_dec.astype(BF16), sb) + _dot(qk.astype(BF16), ub)
    gl_rows = rows_cat([jnp.broadcast_to(jnp.exp(_lane_col(last, SM_A + h)), lead + (HEAD_DIM, 1))
                        for h in range(C_HEADS)])
    s_new = s_bd * gl_rows + _dot(jnp.swapaxes(k_dec, -1, -2).astype(BF16), ub)
    o_c = o[..., 0:c, :]
    for h in range(1, C_HEADS):
        o_c = o_c + o[..., h * c:(h + 1) * c, :]
    return o_c, s_new


def _dn_prep_kernel(x_ref, small_ref, cw_ref, alog_ref, dtb_ref, q_ref, k_ref, v_ref, g_ref, bt_ref, xp_ref, *, seq):
    keep = DN_CONV - 1
    xp_ref[0:DN_PAD, :] = jnp.zeros((DN_PAD, 3 * C_WIDTH), F32)
    xp_ref[DN_PAD:DN_PAD + seq, :] = x_ref[...]
    ones_bd = _head_sum_matrix()
    sm = small_ref[...]
    g_ref[...] = -jnp.exp(alog_ref[...]) * _softplus(sm + dtb_ref[...])
    bt_ref[...] = _sigmoid(sm)

    shift = DN_PAD - keep
    prep = _divisor_rows(seq, DN_PREP_ROWS)
    for c in range(seq // prep):
        r0 = c * prep
        acc = None
        for w in range(DN_CONV):
            t = cw_ref[w:w + 1, :] * xp_ref[r0 + shift + w:r0 + shift + w + prep, :]
            acc = t if acc is None else acc + t
        y = _silu(acc)
        qq, kk, vv = y[:, :C_WIDTH], y[:, C_WIDTH:2 * C_WIDTH], y[:, 2 * C_WIDTH:]
        rows = slice(r0, r0 + prep)
        q_ref[rows, :] = qq * lax.rsqrt(_head_sumsq(qq, ones_bd) + EPS) * (HEAD_DIM ** -0.5)
        k_ref[rows, :] = kk * lax.rsqrt(_head_sumsq(kk, ones_bd) + EPS)
        v_ref[rows, :] = vv


def _dn_scan_kernel(q_ref, k_ref, v_ref, g_ref, bt_ref, z_ref, ng_ref, oc_ref, srec_ref, *, seq, group):
    def run_chunk(r0, chunk, states):
        rows = [pl.ds(b * seq + r0, chunk) for b in range(group)]
        stack = lambda ref: jnp.stack([ref[r, :] for r in rows])
        o, states = _delta_chunk(stack(q_ref), stack(k_ref), stack(v_ref), stack(g_ref), stack(bt_ref),
                                 states, chunk)
        for b, r in enumerate(rows):
            oc_ref[r, :] = o[b]
        return states

    states = run_chunk(0, NUM_META, jnp.zeros((group, C_WIDTH, C_WIDTH), F32))

    def body(i, states):
        return run_chunk(pl.multiple_of(NUM_META + i * DN_CHUNK, SUBLANES), DN_CHUNK, states)

    states = lax.fori_loop(0, (seq - NUM_META) // DN_CHUNK, body, states)
    for b in range(group):
        s_bd = states[b]
        half = s_bd[:, 0:LANES] + s_bd[:, LANES:2 * LANES]
        fold = half + pltpu.roll(half, HEAD_DIM, axis=1)
        srec_ref[b] = fold[:, 0:HEAD_DIM].reshape(C_HEADS, HEAD_DIM, HEAD_DIM)

    ones_bd = _head_sum_matrix()
    prep = _divisor_rows(seq, DN_PREP_ROWS)
    for c in range(group * seq // prep):
        rows = slice(c * prep, (c + 1) * prep)
        o = oc_ref[rows, :]
        ms = _head_sumsq(o, ones_bd) * (1.0 / HEAD_DIM)
        oc_ref[rows, :] = o * lax.rsqrt(ms + EPS) * ng_ref[...] * _silu(z_ref[rows, :])


def _deltanet(qkvc, z, small, cw, alog, dtb, ng, batch, seq):
    n = batch * seq
    vec = _const_spec((1, LANES))
    rows = lambda width: pl.BlockSpec((seq, width), lambda b: (b, 0))
    f = lambda width: jax.ShapeDtypeStruct((n, width), F32)
    q, k, v, g, bt = pl.pallas_call(
        functools.partial(_dn_prep_kernel, seq=seq),
        grid=(batch,),
        in_specs=[rows(3 * C_WIDTH), rows(LANES), _const_spec(cw.shape), vec, vec],
        out_specs=[rows(C_WIDTH), rows(C_WIDTH), rows(C_WIDTH), rows(LANES), rows(LANES)],
        out_shape=[f(C_WIDTH), f(C_WIDTH), f(C_WIDTH), f(LANES), f(LANES)],
        scratch_shapes=[pltpu.VMEM((DN_PAD + seq, 3 * C_WIDTH), F32)],
        compiler_params=_cparams(("parallel",)),
        name="dn_prep",
    )(qkvc, small, cw, alog, dtb)

    group = 2 if batch % 2 == 0 else 1
    once = lambda width: pl.BlockSpec((group * seq, width), lambda b: (b, 0), pipeline_mode=pl.Buffered(1))
    return pl.pallas_call(
        functools.partial(_dn_scan_kernel, seq=seq, group=group),
        grid=(batch // group,),
        in_specs=[once(C_WIDTH), once(C_WIDTH), once(C_WIDTH), once(LANES), once(LANES), once(C_WIDTH),
                  _const_spec((1, C_WIDTH))],
        out_specs=[pl.BlockSpec((group * seq, C_WIDTH), lambda b: (b, 0)),
                   pl.BlockSpec((group, C_HEADS, HEAD_DIM, HEAD_DIM), lambda b: (b, 0, 0, 0))],
        out_shape=[f(C_WIDTH), jax.ShapeDtypeStruct((batch, C_HEADS, HEAD_DIM, HEAD_DIM), F32)],
        compiler_params=_cparams(("parallel",)),
        name="dn_scan",
    )(q, k, v, g, bt, z, ng)


def _out_ffn_kernel(x_ref, oa_ref, ob_ref, oc_ref, ga_ref, gb_ref, wo_ref, gf_ref, wg_ref, wu_ref, wd_ref,
                    fin_ref, y_ref, *, final):
    oa = jnp.concatenate([oa_ref[p] for p in range(A_PAIRS)], axis=-1)
    mixed = jnp.concatenate([_rms(oa, ga_ref[...]), _rms(ob_ref[...], gb_ref[...]), oc_ref[...]], axis=-1)
    x1 = x_ref[...] + _dot(mixed.astype(BF16), wo_ref[...])
    xn = _rms(x1, gf_ref[...]).astype(BF16)
    hid = _silu(_dot(xn, wg_ref[...])) * _dot(xn, wu_ref[...])
    x2 = x1 + _dot(hid.astype(BF16), wd_ref[...])
    y_ref[...] = _rms(x2, fin_ref[...]) if final else x2


def _out_ffn(x, oa, ob, oc, ga, gb, wo, gf, wg, wu, wd, fin, tm, final):
    n, d = x.shape
    row = lambda width: pl.BlockSpec((tm, width), lambda i: (i, 0))
    return pl.pallas_call(
        functools.partial(_out_ffn_kernel, final=final),
        grid=(n // tm,),
        in_specs=[row(d), pl.BlockSpec((A_PAIRS, tm, LANES), lambda i: (0, i, 0)), row(B_WIDTH),
                  row(C_WIDTH), _const_spec(ga.shape), _const_spec(gb.shape), _const_spec(wo.shape),
                  _const_spec(gf.shape), _const_spec(wg.shape), _const_spec(wu.shape), _const_spec(wd.shape),
                  _const_spec(fin.shape)],
        out_specs=row(d),
        out_shape=jax.ShapeDtypeStruct((n, d), F32),
        compiler_params=_cparams(("parallel",)),
        name="out_ffn",
    )(x, oa, ob, oc, ga, gb, wo, gf, wg, wu, wd, fin)


POOL_BLOCK = 256
PAGES_PER_STEP = 32


def _pool_suffix_kernel(lf_ref, out_ref):
    pages, heads, rows = lf_ref.shape
    x = lf_ref[...].reshape(pages * heads, rows)
    parts = _split3(x)
    later = (_iota((rows, rows), 0) > _iota((rows, rows), 1)).astype(BF16)
    ones = jnp.ones((rows, rows), BF16)
    excl = tot = None
    for part in parts:
        e, t = _dot(part, later), _dot(part, ones)
        excl = e if excl is None else excl + e
        tot = t if tot is None else tot + t
    out_ref[:, 0:heads, :] = excl.reshape(pages, heads, rows)
    out_ref[:, heads:2 * heads, :] = tot.reshape(pages, heads, rows)


def _pool_suffix(logf_t):
    depth, n_pool, heads, rows = logf_t.shape
    return pl.pallas_call(
        _pool_suffix_kernel,
        grid=(depth, pl.cdiv(n_pool, POOL_BLOCK)),
        in_specs=[pl.BlockSpec((None, POOL_BLOCK, heads, rows), lambda l, i: (l, i, 0, 0))],
        out_specs=pl.BlockSpec((None, POOL_BLOCK, 2 * heads, rows), lambda l, i: (l, i, 0, 0)),
        out_shape=jax.ShapeDtypeStruct((depth, n_pool, 2 * heads, rows), F32),
        compiler_params=_cparams(("parallel", "parallel")),
        name="pool_suffix",
    )(logf_t)


def _fox_decode_kernel(pt_ref, q_ref, kn_ref, vn_ref, cn_ref, *rest):
    pps = PAGES_PER_STEP
    k_refs, v_refs, r_refs = rest[0:pps], rest[pps:2 * pps], rest[2 * pps:3 * pps]
    o_ref, m_ref, l_ref, acc_ref, suf_ref = rest[3 * pps:]
    j = pl.program_id(1)
    own = (_iota((A_HEADS, A_WIDTH), 1) // HEAD_DIM) == _iota((A_HEADS, A_WIDTH), 0)
    qbd = jnp.where(own, q_ref[...], 0.0)

    @pl.when(j == 0)
    def _():
        m_ref[...] = jnp.broadcast_to(jnp.sum(qbd * kn_ref[...], axis=1, keepdims=True), m_ref.shape)
        l_ref[...] = jnp.ones(l_ref.shape, F32)
        acc_ref[...] = jnp.broadcast_to(vn_ref[...], acc_ref.shape)
        suf_ref[...] = jnp.zeros(suf_ref.shape, F32)

    qb = qbd.astype(BF16)
    cn = cn_ref[...]
    m, l, acc, suf = m_ref[...], l_ref[...], acc_ref[...], suf_ref[...]

    def page_t(ref):
        return ref[...].reshape(A_WIDTH, ref.shape[-1]).astype(BF16)

    scores = []
    for u in range(pps):
        r = r_refs[u][...]
        s = (cn + suf + r[0:A_HEADS]) * LOG2E + _dot(qb, page_t(k_refs[u]))
        suf = suf + r[A_HEADS:2 * A_HEADS]
        scores.append(s)
    smax = scores[0]
    for s in scores[1:]:
        smax = jnp.maximum(smax, s)
    m_new = jnp.maximum(m, jnp.max(smax, axis=1, keepdims=True))
    a = jnp.exp2(m - m_new)
    probs = [jnp.exp2(s - m_new) for s in scores]
    psum = probs[0]
    for p in probs[1:]:
        psum = psum + p
    l = a * l + jnp.sum(psum, axis=1, keepdims=True)
    acc = acc * a[:, 0:1]
    for u in range(pps):
        acc = acc + _dot_nt(probs[u].astype(BF16), page_t(v_refs[u]))
    m_ref[...], l_ref[...], acc_ref[...], suf_ref[...] = m_new, l, acc, suf

    @pl.when(j == pl.num_programs(1) - 1)
    def _():
        o_ref[...] = jnp.sum(jnp.where(own, acc / l[:, 0:1], 0.0), axis=0, keepdims=True)


def _fox_decode(page_table, layer, q, k_new, v_new, logf_new, cache_kt, cache_vt, suffix):
    db, n_pages = page_table.shape
    pps = PAGES_PER_STEP
    assert n_pages % pps == 0
    steps = n_pages // pps
    rows = cache_kt.shape[-1]
    tok = pl.BlockSpec((None, 1, A_WIDTH), lambda b, j, pt: (b, 0, 0))

    def page(u, shape):
        zeros = (0,) * len(shape)
        return pl.BlockSpec((None, None) + shape,
                            lambda b, j, pt: (layer, pt[b, n_pages - 1 - (j * pps + u)]) + zeros)

    grid_spec = pltpu.PrefetchScalarGridSpec(
        num_scalar_prefetch=1,
        grid=(db, steps),
        in_specs=[tok, tok, tok, pl.BlockSpec((None, A_HEADS, LANES), lambda b, j, pt: (b, 0, 0))]
                 + [page(u, (A_HEADS, HEAD_DIM, rows)) for u in range(pps)]
                 + [page(u, (A_HEADS, HEAD_DIM, rows)) for u in range(pps)]
                 + [page(u, (2 * A_HEADS, rows)) for u in range(pps)],
        out_specs=tok,
        scratch_shapes=[pltpu.VMEM((A_HEADS, LANES), F32), pltpu.VMEM((A_HEADS, LANES), F32),
                        pltpu.VMEM((A_HEADS, A_WIDTH), F32), pltpu.VMEM((A_HEADS, LANES), F32)],
    )
    tok3 = lambda t: t.reshape(db, 1, A_WIDTH)
    return pl.pallas_call(
        _fox_decode_kernel,
        grid_spec=grid_spec,
        out_shape=jax.ShapeDtypeStruct((db, 1, A_WIDTH), F32),
        compiler_params=_cparams(("parallel", "arbitrary")),
        name="fox_decode",
    )(page_table, tok3(q), tok3(k_new), tok3(v_new), logf_new, *([cache_kt] * pps), *([cache_vt] * pps),
      *([suffix] * pps))


def _sample_mix_kernel(h_ref, cbuf_ref, dw_ref, dwb_ref, lng_ref, lnb_ref, pw_ref, pwb_ref,
                       x_ref, dbuf_ref, cw_ref, small_ref, alog_ref, dtb_ref,
                       ob_ref, q_ref, k_ref, v_ref, eg_ref, bt_ref, lf_ref):
    keep = CONF_WIDTH - 1
    h = h_ref[...]
    cbuf = cbuf_ref[...]
    acc = jnp.sum(cbuf * dw_ref[0:keep, :][None], axis=1) + dw_ref[keep:keep + 1, :] * h + dwb_ref[...]
    mu = jnp.mean(acc, axis=-1, keepdims=True)
    xc = acc - mu
    var = jnp.mean(xc * xc, axis=-1, keepdims=True)
    y = xc * lax.rsqrt(var + EPS) * lng_ref[...] + lnb_ref[...]
    ob_ref[...] = _dot(_silu(y).astype(BF16), pw_ref[...]) + pwb_ref[...]

    dk = DN_CONV - 1
    x = x_ref[...]
    dbuf = dbuf_ref[...]
    conv = jnp.sum(dbuf * cw_ref[0:dk, :][None], axis=1) + cw_ref[dk:dk + 1, :] * x
    yv = _silu(conv)
    qq, kk, vv = yv[:, :C_WIDTH], yv[:, C_WIDTH:2 * C_WIDTH], yv[:, 2 * C_WIDTH:]
    ones_bd = _head_sum_matrix()
    q_ref[...] = qq * lax.rsqrt(_head_sumsq(qq, ones_bd) + EPS) * (HEAD_DIM ** -0.5)
    k_ref[...] = kk * lax.rsqrt(_head_sumsq(kk, ones_bd) + EPS)
    v_ref[...] = vv
    sm = small_ref[...]
    eg_ref[...] = jnp.exp(-jnp.exp(alog_ref[...]) * _softplus(sm + dtb_ref[...]))
    bt_ref[...] = _sigmoid(sm)
    lf_ref[...] = _log_sigmoid(sm)


def _sample_mix(h, cbuf, dw, dwb, lng, lnb, pw, pwb, x, dbuf, cw, small, alog, dtb):
    db = h.shape[0]
    f = lambda *shape: jax.ShapeDtypeStruct(shape, F32)
    return pl.pallas_call(
        _sample_mix_kernel,
        out_shape=[f(db, B_WIDTH), f(db, C_WIDTH), f(db, C_WIDTH), f(db, C_WIDTH),
                   f(db, LANES), f(db, LANES), f(db, LANES)],
        compiler_params=pltpu.CompilerParams(vmem_limit_bytes=VMEM_LIMIT_BYTES),
        name="sample_mix",
    )(h, cbuf, dw, dwb, lng, lnb, pw, pwb, x, dbuf, cw, small, alog, dtb)


def _sample_delta_kernel(s_ref, qc_ref, kc_ref, qr_ref, kr_ref, v_ref, eg_ref, bt_ref, z_ref, ng_ref,
                         o_ref, snew_ref):
    sd = s_ref[...] * eg_ref[...]
    ks = jnp.sum(kc_ref[...] * sd, axis=1, keepdims=True)
    u = bt_ref[...] * (v_ref[...] - ks)
    qs = jnp.sum(qc_ref[...] * sd, axis=1, keepdims=True)
    qk = jnp.sum(qr_ref[...] * kr_ref[...], axis=2, keepdims=True)
    o = qs + qk * u
    snew_ref[...] = sd + kc_ref[...] * u
    ms = jnp.mean(o * o, axis=-1, keepdims=True)
    o_ref[...] = o * lax.rsqrt(ms + EPS) * ng_ref[...] * _silu(z_ref[...])


def _sample_delta(s, qn, kn, vn, eg, bt, z, ng):
    db = s.shape[0]
    g = db * C_HEADS
    col = lambda t: t.reshape(g, HEAD_DIM, 1)
    row = lambda t: t.reshape(g, 1, HEAD_DIM)
    gate = lambda t, off: t[:, off:off + C_HEADS].reshape(g, 1, 1)
    o, s_new = pl.pallas_call(
        _sample_delta_kernel,
        out_shape=[jax.ShapeDtypeStruct((g, 1, HEAD_DIM), F32), jax.ShapeDtypeStruct((g, HEAD_DIM, HEAD_DIM), F32)],
        compiler_params=pltpu.CompilerParams(vmem_limit_bytes=VMEM_LIMIT_BYTES),
        name="sample_delta",
    )(s.reshape(g, HEAD_DIM, HEAD_DIM), col(qn), col(kn), row(qn), row(kn), row(vn), gate(eg, SM_A), gate(bt, SM_B),
      row(z), ng.reshape(1, 1, HEAD_DIM))
    return o.reshape(db, C_WIDTH), s_new.reshape(db, C_HEADS, HEAD_DIM, HEAD_DIM)


def _permute_in_cols(t):
    aw, bw, cw = A_WIDTH, B_WIDTH, C_WIDTH
    f0 = 3 * aw
    glu0 = f0 + A_HEADS
    qkv0 = glu0 + 2 * bw
    a0 = qkv0 + 3 * cw
    b0 = a0 + C_HEADS
    z0 = b0 + C_HEADS
    parts = [t[..., 0:f0], t[..., glu0:qkv0], t[..., qkv0:a0], t[..., z0:z0 + cw],
             t[..., f0:glu0], t[..., a0:b0], t[..., b0:z0]]
    out = jnp.concatenate(parts, axis=-1)
    return jnp.pad(out, [(0, 0)] * (t.ndim - 1) + [(0, IN_COLS_PAD - out.shape[-1])])


def _lane_vec(vals, off):
    return jnp.zeros((1, LANES), F32).at[0, off:off + vals.shape[0]].set(vals)


def _row_tile(n):
    for tm in (384, 256, 128, 64, 32, 16, 8):
        if n % tm == 0:
            return tm
    raise ValueError(n)


def kernel(x_prompt, x_sample, cache_k, cache_v, cache_logf, state_conf_conv, state_dn_conv, state_dn_rec, page_table, meta_tokens, norm_mix, w_in, b_in, conf_dw, conf_dw_b, conf_ln_g, conf_ln_b, conf_pw, conf_pw_b, dn_conv, dn_a_log, dn_dt_bias, dn_norm_g, grp_norm_a, grp_norm_b, w_out, norm_ffn, w_ffn_gate, w_ffn_up, w_ffn_down, final_norm):
    batch, seq_in, d = x_prompt.shape
    db = x_sample.shape[0]
    assert x_sample.shape[1] == 1
    depth = w_in.shape[0]
    seq = NUM_META + seq_in
    n = batch * seq

    xp = jnp.concatenate([jnp.broadcast_to(meta_tokens[None], (batch, NUM_META, d)), x_prompt], axis=1).reshape(n, d)
    xs = x_sample.reshape(db, d)
    w_in_p = _permute_in_cols(w_in).astype(BF16)
    b_in_p = _permute_in_cols(b_in)
    tm_p, tm_s = _row_tile(n), _row_tile(db)
    fin = final_norm.reshape(1, d)
    r1 = lambda t: t.reshape(1, -1)
    suffix = _pool_suffix(jnp.transpose(cache_logf, (0, 1, 3, 2)))
    cache_kt = jnp.transpose(cache_k, (0, 1, 3, 4, 2))
    cache_vt = jnp.transpose(cache_v, (0, 1, 3, 4, 2))

    p_out = [[] for _ in range(6)]
    s_out = [[] for _ in range(6)]
    for l in range(depth):
        g_mix, g_ffn = r1(norm_mix[l]), r1(norm_ffn[l])
        wo, wg, wu, wd = (w_out[l].astype(BF16), w_ffn_gate[l].astype(BF16), w_ffn_up[l].astype(BF16),
                          w_ffn_down[l].astype(BF16))
        pw = conf_pw[l].astype(BF16)
        alog, dtb = _lane_vec(dn_a_log[l], SM_A), _lane_vec(dn_dt_bias[l], SM_A)
        ng_t = jnp.tile(dn_norm_g[l], C_HEADS).reshape(1, C_WIDTH)
        last = l == depth - 1
        conf_args = (conf_dw[l], r1(conf_dw_b[l]), r1(conf_ln_g[l]), r1(conf_ln_b[l]), pw, r1(conf_pw_b[l]))
        ffn_args = (r1(grp_norm_a[l]), r1(grp_norm_b[l]), wo, g_ffn, wg, wu, wd, fin)

        qb, kb, vb, _, kf, vf, h, qkvc, z, small = _inproj(xp, g_mix, w_in_p[l], b_in_p[l:l + 1], tm_p)
        logf, qaug, kaug = _fox_gates(small, batch, seq)
        oa = _fox_prompt(qb, kb, vb, qaug, kaug, batch, seq)
        ob = _conformer(h, *conf_args, batch, seq)
        conf_buf = h.reshape(batch, seq, B_WIDTH)[:, seq - (CONF_WIDTH - 1):]
        dn_buf = qkvc.reshape(batch, seq, 3 * C_WIDTH)[:, seq - (DN_CONV - 1):]
        oc, s_rec = _deltanet(qkvc, z, small, dn_conv[l], alog, dtb, ng_t, batch, seq)
        xp = _out_ffn(xp, oa, ob, oc, *ffn_args, tm_p, last)
        for i, t in enumerate((kf.reshape(batch, seq, A_HEADS, HEAD_DIM), vf.reshape(batch, seq, A_HEADS, HEAD_DIM),
                               logf, conf_buf, dn_buf, s_rec)):
            p_out[i].append(t)

        _, _, _, qf, kf, vf, h, qkvc, z, small = _inproj(xs, g_mix, w_in_p[l], b_in_p[l:l + 1], tm_s)
        conf_new = jnp.concatenate([state_conf_conv[l][:, 1:], h[:, None]], axis=1)
        dn_new = jnp.concatenate([state_dn_conv[l][:, 1:], qkvc[:, None]], axis=1)
        ob, qn, kn, vn, eg, bt, lf = _sample_mix(
            h, state_conf_conv[l], *conf_args, qkvc, state_dn_conv[l], dn_conv[l], small, alog, dtb)
        lf_heads = lf[:, SM_F:SM_F + A_HEADS]
        cn = jnp.broadcast_to(lf_heads[:, :, None], (db, A_HEADS, LANES))
        oa = _fox_decode(page_table, l, qf, kf, vf, cn, cache_kt, cache_vt, suffix)
        oa = jnp.transpose(oa.reshape(db, A_PAIRS, LANES), (1, 0, 2))
        oc, s_new = _sample_delta(state_dn_rec[l], qn, kn, vn, eg, bt, z, dn_norm_g[l])
        xs = _out_ffn(xs, oa, ob, oc, *ffn_args, tm_s, last)
        for i, t in enumerate((kf.reshape(db, 1, A_HEADS, HEAD_DIM), vf.reshape(db, 1, A_HEADS, HEAD_DIM),
                               lf_heads.reshape(db, 1, A_HEADS), conf_new, dn_new, s_new)):
            s_out[i].append(t)

    y_prompt = xp.reshape(batch, seq, d)[:, NUM_META:]
    y_sample = xs.reshape(db, 1, d)
    return (y_prompt, y_sample, *[jnp.stack(a) for a in p_out], *[jnp.stack(a) for a in s_out])
```

```python
import functools

import jax
import jax.numpy as jnp
from jax import lax
from jax.experimental import pallas as pl
from jax.experimental.pallas import tpu as pltpu

F32 = jnp.float32
BF16 = jnp.bfloat16
HIGHEST = lax.Precision.HIGHEST

LANES = 128
SUBLANES = 8
VMEM_LIMIT_BYTES = 56 * 1024 * 1024

NUM_META = 16
A_HEADS = 8
HEAD_DIM = 64
A_WIDTH = A_HEADS * HEAD_DIM
A_PAIRS = A_WIDTH // LANES
B_WIDTH = 256
CONF_WIDTH = 31
C_HEADS = 4
C_WIDTH = C_HEADS * HEAD_DIM
DN_CONV = 4
DN_CHUNK = 64
EPS = 1e-6
NEG_BIG = -1e30
LOG2E = 1.4426950408889634

_Q0, _K0, _V0, _GLU0, _QKVC0, _Z0, _SM0 = 0, 512, 1024, 1536, 2048, 2816, 3072
IN_COLS_PAD = 3200
SM_F, SM_A, SM_B = 0, 8, 12


def _cparams(sem):
    return pltpu.CompilerParams(dimension_semantics=sem, vmem_limit_bytes=VMEM_LIMIT_BYTES)


def _const_spec(shape):
    nd = len(shape)
    return pl.BlockSpec(shape, lambda *_: (0,) * nd, pipeline_mode=pl.Buffered(1))


def _batched(a, b):
    if a.ndim == 2:
        a = jnp.broadcast_to(a, b.shape[:1] + a.shape)
    if b.ndim == 2:
        b = jnp.broadcast_to(b, a.shape[:1] + b.shape)
    return a, b


def _dot(a, b, precision=None):
    if a.ndim == 3 or b.ndim == 3:
        a, b = _batched(a, b)
        return lax.dot_general(a, b, (((2,), (1,)), ((0,), (0,))), preferred_element_type=F32, precision=precision)
    return jnp.dot(a, b, preferred_element_type=F32, precision=precision)


def _dot_nt(a, b, precision=None):
    if a.ndim == 3 or b.ndim == 3:
        a, b = _batched(a, b)
        return lax.dot_general(a, b, (((2,), (2,)), ((0,), (0,))), preferred_element_type=F32, precision=precision)
    return lax.dot_general(a, b, (((1,), (1,)), ((), ())), preferred_element_type=F32, precision=precision)


def _sigmoid(x):
    return 1.0 / (1.0 + jnp.exp(-x))


def _silu(x):
    return x * _sigmoid(x)


def _log_sigmoid(x):
    return jnp.minimum(x, 0.0) - jnp.log1p(jnp.exp(-jnp.abs(x)))


def _softplus(x):
    return jnp.maximum(x, 0.0) + jnp.log1p(jnp.exp(-jnp.abs(x)))


def _rms(x, g):
    return x * lax.rsqrt(jnp.mean(x * x, axis=-1, keepdims=True) + EPS) * g


def _iota(shape, dim):
    return lax.broadcasted_iota(jnp.int32, shape, dim)


def _split2(x):
    hi = x.astype(BF16)
    return hi, (x - hi.astype(F32)).astype(BF16)


def _split3(x):
    hi = x.astype(BF16)
    r = x - hi.astype(F32)
    mid = r.astype(BF16)
    return hi, mid, (r - mid.astype(F32)).astype(BF16)


def _dot_split(a, b):
    (ah, al), (bh, bl) = a, b
    return _dot(ah, bh) + (_dot(ah, bl) + _dot(al, bh))


def _lane_col(x, lane):
    return jnp.sum(jnp.where(_iota((1, x.shape[-1]), 1) == lane, x, 0.0), axis=-1, keepdims=True)


def _inproj_kernel(x_ref, g_ref, w_ref, b_ref, qb_ref, kb_ref, vb_ref, qf_ref, kf_ref, vf_ref,
                   h_ref, qkvc_ref, z_ref, small_ref):
    xn = _rms(x_ref[...], g_ref[...]).astype(BF16)

    def seg(lo, hi):
        return _dot(xn, w_ref[:, lo:hi]) + b_ref[:, lo:hi]

    q = seg(_Q0, _K0) * (HEAD_DIM ** -0.5 * LOG2E)
    k = seg(_K0, _V0)
    v = seg(_V0, _GLU0)
    qf_ref[...] = q
    kf_ref[...] = k
    vf_ref[...] = v
    for p in range(A_PAIRS):
        sl = slice(p * LANES, (p + 1) * LANES)
        qb_ref[p] = q[:, sl].astype(BF16)
        kb_ref[p] = k[:, sl].astype(BF16)
        vb_ref[p] = v[:, sl].astype(BF16)
    glu = seg(_GLU0, _QKVC0)
    h_ref[...] = glu[:, :B_WIDTH] * _sigmoid(glu[:, B_WIDTH:])
    qkvc_ref[...] = seg(_QKVC0, _Z0)
    z_ref[...] = seg(_Z0, _SM0)
    small_ref[...] = seg(_SM0, IN_COLS_PAD)


def _inproj(x, g, w, b, tm):
    n, d = x.shape
    row = lambda width: pl.BlockSpec((tm, width), lambda i: (i, 0))
    pair = pl.BlockSpec((A_PAIRS, tm, LANES), lambda i: (0, i, 0))
    pair_shape = jax.ShapeDtypeStruct((A_PAIRS, n, LANES), BF16)
    f = lambda width: jax.ShapeDtypeStruct((n, width), F32)
    return pl.pallas_call(
        _inproj_kernel,
        grid=(n // tm,),
        in_specs=[row(d), _const_spec((1, d)), _const_spec(w.shape), _const_spec(b.shape)],
        out_specs=[pair, pair, pair, row(A_WIDTH), row(A_WIDTH), row(A_WIDTH), row(B_WIDTH),
                   row(3 * C_WIDTH), row(C_WIDTH), row(LANES)],
        out_shape=[pair_shape, pair_shape, pair_shape, f(A_WIDTH), f(A_WIDTH), f(A_WIDTH), f(B_WIDTH),
                   f(3 * C_WIDTH), f(C_WIDTH), f(LANES)],
        compiler_params=_cparams(("parallel",)),
        name="inproj",
    )(x, g, w, b)


AUG_PARTS = 3
AUG_HEAD = 2 * AUG_PARTS


def _fox_gates_kernel(small_ref, logf_ref, qaug_ref, kaug_ref, c_ref, *, seq):
    lf = _log_sigmoid(small_ref[...])
    logf_ref[...] = lf[:, SM_F:SM_F + A_HEADS]
    tril = (_iota((LANES, LANES), 0) >= _iota((LANES, LANES), 1)).astype(F32)
    carry = jnp.zeros((1, LANES), F32)
    full = seq // LANES
    for j in range(full):
        cs = _dot(tril, lf[j * LANES:(j + 1) * LANES], HIGHEST) + carry
        c_ref[j * LANES:(j + 1) * LANES, :] = cs
        carry = cs[LANES - 1:LANES]
    tail = seq - full * LANES
    if tail:
        c_ref[full * LANES:seq, :] = _dot(tril[:tail, :tail], lf[full * LANES:seq], HIGHEST) + carry

    parts = _split3(c_ref[...] * LOG2E)
    row, col = _iota((LANES, LANES), 0), _iota((LANES, LANES), 1)
    p3 = None
    for i, part in enumerate(parts):
        t = _dot(part, ((col == AUG_PARTS * row + i) & (row < A_HEADS)).astype(BF16))
        p3 = t if p3 is None else p3 + t
    p3 = p3.astype(BF16)
    lane = _iota((1, LANES), 1)
    slot = lane // AUG_PARTS
    for p in range(A_PAIRS):
        src = AUG_HEAD * p + (col // AUG_HEAD) * AUG_PARTS + col % AUG_PARTS
        hit = row == src
        cslot = col // AUG_PARTS
        gq = (hit & ((cslot == 0) | (cslot == 2))).astype(BF16)
        gk = (hit & ((cslot == 1) | (cslot == 3))).astype(BF16)
        one_q = ((slot == 1) | (slot == 3)).astype(F32)
        one_k = ((slot == 0) | (slot == 2)).astype(F32)
        qaug_ref[p] = (_dot(p3, gq) + one_q).astype(BF16)
        kaug_ref[p] = (one_k - _dot(p3, gk)).astype(BF16)


def _fox_gates(small, batch, seq):
    pair = pl.BlockSpec((A_PAIRS, seq, LANES), lambda b: (0, b, 0))
    pair_shape = jax.ShapeDtypeStruct((A_PAIRS, batch * seq, LANES), BF16)
    return pl.pallas_call(
        functools.partial(_fox_gates_kernel, seq=seq),
        grid=(batch,),
        in_specs=[pl.BlockSpec((seq, LANES), lambda b: (b, 0))],
        out_specs=[pl.BlockSpec((None, seq, A_HEADS), lambda b: (b, 0, 0)), pair, pair],
        out_shape=[jax.ShapeDtypeStruct((batch, seq, A_HEADS), F32), pair_shape, pair_shape],
        scratch_shapes=[pltpu.VMEM((seq, LANES), F32)],
        compiler_params=_cparams(("parallel",)),
        name="fox_gates",
    )(small)


Q_ROWS = 256


def _fox_prompt_kernel(q_ref, k_ref, v_ref, qa_ref, ka_ref, o_ref, kcat_ref, v0_ref, v1_ref, *, seq):
    lane = _iota((1, LANES), 1)
    lo = lane < HEAD_DIM
    zero = jnp.zeros((), BF16)
    kcat_ref[:, 0:LANES] = k_ref[...]
    kcat_ref[:, LANES:2 * LANES] = ka_ref[...]
    v = v_ref[...]
    v0_ref[...] = jnp.where(lo, v, zero)
    v1_ref[...] = jnp.where(lo, zero, v)
    heads = ((lo, lane < AUG_HEAD, v0_ref), (~lo, (lane >= AUG_HEAD) & (lane < 2 * AUG_HEAD), v1_ref))

    def block(r0, nq):
        q, qa = q_ref[r0:r0 + nq, :], qa_ref[r0:r0 + nq, :]
        causal = _iota((nq, nq), 0) >= _iota((nq, nq), 1)
        out = None
        for lanes_h, aug_h, vh_ref in heads:
            qc = jnp.concatenate([jnp.where(lanes_h, q, zero), jnp.where(aug_h, qa, zero)], axis=1)
            s_d = jnp.where(causal, _dot_nt(qc, kcat_ref[r0:r0 + nq, :]), NEG_BIG)
            m = jnp.max(s_d, axis=1, keepdims=True)
            if r0:
                s_f = _dot_nt(qc, kcat_ref[0:r0, :])
                m = jnp.maximum(m, jnp.max(s_f, axis=1, keepdims=True))
            p_d = jnp.exp2(s_d - m)
            l = jnp.sum(p_d, axis=1, keepdims=True)
            o = _dot(p_d.astype(BF16), vh_ref[r0:r0 + nq, :])
            if r0:
                p_f = jnp.exp2(s_f - m)
                l = l + jnp.sum(p_f, axis=1, keepdims=True)
                o = o + _dot(p_f.astype(BF16), vh_ref[0:r0, :])
            o = o * (1.0 / l)
            out = o if out is None else out + o
        o_ref[r0:r0 + nq, :] = out

    full = seq // Q_ROWS
    for i in range(full):
        block(i * Q_ROWS, Q_ROWS)
    if seq - full * Q_ROWS:
        block(full * Q_ROWS, seq - full * Q_ROWS)


def _fox_prompt(qb, kb, vb, qaug, kaug, batch, seq):
    n = batch * seq
    tile = pl.BlockSpec((None, seq, LANES), lambda b, p: (p, b, 0))
    return pl.pallas_call(
        functools.partial(_fox_prompt_kernel, seq=seq),
        grid=(batch, A_PAIRS),
        in_specs=[tile] * 5,
        out_specs=tile,
        out_shape=jax.ShapeDtypeStruct((A_PAIRS, n, LANES), F32),
        scratch_shapes=[pltpu.VMEM((seq, 2 * LANES), BF16), pltpu.VMEM((seq, LANES), BF16),
                        pltpu.VMEM((seq, LANES), BF16)],
        compiler_params=_cparams(("parallel", "parallel")),
        name="fox_prompt",
    )(qb, kb, vb, qaug, kaug)


CONF_ROWS = 48
CONF_PAD = 32


def _conformer_kernel(h_ref, dw_ref, dwb_ref, lng_ref, lnb_ref, pw_ref, pwb_ref, ob_ref, hp_ref, *, seq):
    hp_ref[0:CONF_PAD, :] = jnp.zeros((CONF_PAD, B_WIDTH), F32)
    hp_ref[CONF_PAD:CONF_PAD + seq, :] = h_ref[...]
    shift = CONF_PAD - (CONF_WIDTH - 1)

    def chunk(c):
        r0 = c * CONF_ROWS
        if not isinstance(r0, int):
            r0 = pl.multiple_of(r0, SUBLANES)
        win = hp_ref[pl.ds(r0, CONF_ROWS + CONF_PAD), :]
        rows = CONF_ROWS + CONF_PAD
        phases = [win] + [pltpu.roll(win, rows - r, axis=0) for r in range(1, SUBLANES)]
        acc = jnp.zeros((CONF_ROWS, B_WIDTH), F32) + dwb_ref[...]
        for w in range(CONF_WIDTH):
            base, r = (shift + w) // SUBLANES * SUBLANES, (shift + w) % SUBLANES
            acc = acc + dw_ref[w:w + 1, :] * phases[r][base:base + CONF_ROWS, :]
        mu = jnp.mean(acc, axis=-1, keepdims=True)
        xc = acc - mu
        var = jnp.mean(xc * xc, axis=-1, keepdims=True)
        y = xc * lax.rsqrt(var + EPS) * lng_ref[...] + lnb_ref[...]
        ob_ref[pl.ds(r0, CONF_ROWS), :] = _dot(_silu(y).astype(BF16), pw_ref[...]) + pwb_ref[...]

    def pair(i, carry):
        chunk(2 * i)
        chunk(2 * i + 1)
        return carry

    n_chunks = seq // CONF_ROWS
    lax.fori_loop(0, n_chunks // 2, pair, 0)
    if n_chunks % 2:
        chunk(n_chunks - 1)


def _conformer(h, dw, dwb, lng, lnb, pw, pwb, batch, seq):
    assert seq % CONF_ROWS == 0
    vec = _const_spec((1, B_WIDTH))
    return pl.pallas_call(
        functools.partial(_conformer_kernel, seq=seq),
        grid=(batch,),
        in_specs=[pl.BlockSpec((seq, B_WIDTH), lambda b: (b, 0)), _const_spec(dw.shape), vec, vec, vec,
                  _const_spec(pw.shape), vec],
        out_specs=pl.BlockSpec((seq, B_WIDTH), lambda b: (b, 0)),
        out_shape=jax.ShapeDtypeStruct((batch * seq, B_WIDTH), F32),
        scratch_shapes=[pltpu.VMEM((CONF_PAD + seq, B_WIDTH), F32)],
        compiler_params=_cparams(("parallel",)),
        name="conformer",
    )(h, dw, dwb, lng, lnb, pw, pwb)


DN_PAD = 8
DN_PREP_ROWS = 344


def _divisor_rows(seq, cap):
    return max(r for r in range(SUBLANES, cap + 1, SUBLANES) if seq % r == 0)


def _head_sum_matrix():
    return ((_iota((C_WIDTH, C_WIDTH), 0) // HEAD_DIM) == (_iota((C_WIDTH, C_WIDTH), 1) // HEAD_DIM)).astype(BF16)


def _head_sumsq(x, ones_bd):
    hi, lo = _split2(x * x)
    return _dot(hi, ones_bd) + _dot(lo, ones_bd)


def _unit_lower_inverse(ns, size):
    n = ns.shape[-1]
    eye = (_iota((n, n), 0) == _iota((n, n), 1)).astype(F32)
    inv = eye - ns
    pw = ns.astype(BF16)
    k = 2
    while k < size:
        pw = _dot(pw, pw).astype(BF16)
        inv = inv + _dot(inv.astype(BF16), pw)
        k *= 2
    resid = (eye - inv) - _dot_split(_split2(ns), _split2(inv))
    return _split2(inv + _dot(inv.astype(BF16), resid.astype(BF16)))


def _expand_heads(x):
    head = _iota((1, C_WIDTH), 1) // HEAD_DIM
    return jnp.concatenate([jnp.where(head == h, x, 0.0) for h in range(C_HEADS)], axis=-2)


def _delta_chunk(q, k, v, gb, s_bd, chunk):
    g = bt = gb
    c = chunk
    n = C_HEADS * c
    lead = q.shape[:-2]
    tril_c = (_iota((c, c), 0) >= _iota((c, c), 1)).astype(F32)
    gcum = _dot(tril_c, g, HIGHEST)
    last = gcum[..., c - 1:c, :]
    rows_cat = lambda parts: jnp.concatenate(parts, axis=-2)
    gcol = rows_cat([_lane_col(gcum, SM_A + h) for h in range(C_HEADS)])
    bcol = rows_cat([_lane_col(bt, SM_B + h) for h in range(C_HEADS)])
    glast = rows_cat([jnp.broadcast_to(_lane_col(last, SM_A + h), lead + (c, 1)) for h in range(C_HEADS)])
    gmask = rows_cat([jnp.where(_iota((1, LANES), 1) == SM_A + h, gcum, 0.0) for h in range(C_HEADS)])
    ones_b = jnp.ones((n, LANES), BF16)
    grow = None
    for part in _split3(gmask):
        t = _dot_nt(ones_b, part)
        grow = t if grow is None else grow + t
    ri, ci = _iota((n, n), 0), _iota((n, n), 1)
    same = (ri // c) == (ci // c)
    tri = same & (ri >= ci)
    strict = same & (ri > ci)
    decay = jnp.where(tri, jnp.exp(jnp.where(tri, gcol - grow, 0.0)), 0.0)

    kx, qx, vx = _expand_heads(k), _expand_heads(q), _expand_heads(v)
    kb = kx * bcol
    kxb = kx.astype(BF16)
    a_strict = jnp.where(strict, _dot_nt(kb.astype(BF16), kxb) * decay, 0.0)
    inv = _unit_lower_inverse(a_strict, c)
    eg = jnp.exp(gcol)
    sol = _dot_split(inv, _split2(jnp.concatenate([vx * bcol, kb * eg], axis=-1)))
    u0, w = sol[..., :C_WIDTH], sol[..., C_WIDTH:]
    qk = _dot_nt(qx.astype(BF16), kxb) * decay
    q_dec = qx * eg
    k_dec = kx * jnp.exp(glast - gcol)
    sb = s_bd.astype(BF16)
    u = u0 - _dot(w.astype(BF16), sb)
    ub = u.astype(BF16)
    o = _dot(q_dec.astype(BF16), sb) + _dot(qk.astype(BF16), ub)
    gl_rows = rows_cat([jnp.broadcast_to(jnp.exp(_lane_col(last, SM_A + h)), lead + (HEAD_DIM, 1))
                        for h in range(C_HEADS)])
    s_new = s_bd * gl_rows + _dot(jnp.swapaxes(k_dec, -1, -2).astype(BF16), ub)
    o_c = o[..., 0:c, :]
    for h in range(1, C_HEADS):
        o_c = o_c + o[..., h * c:(h + 1) * c, :]
    return o_c, s_new


def _dn_prep_kernel(x_ref, small_ref, cw_ref, alog_ref, dtb_ref, q_ref, k_ref, v_ref, gb_ref, xp_ref, *, seq):
    keep = DN_CONV - 1
    xp_ref[0:DN_PAD, :] = jnp.zeros((DN_PAD, 3 * C_WIDTH), F32)
    xp_ref[DN_PAD:DN_PAD + seq, :] = x_ref[...]
    ones_bd = _head_sum_matrix()
    sm = small_ref[...]
    lane = _iota((1, LANES), 1)
    g = -jnp.exp(alog_ref[...]) * _softplus(sm + dtb_ref[...])
    gb_ref[...] = jnp.where((lane >= SM_A) & (lane < SM_A + C_HEADS), g, _sigmoid(sm))

    shift = DN_PAD - keep
    prep = _divisor_rows(seq, DN_PREP_ROWS)
    for c in range(seq // prep):
        r0 = c * prep
        acc = None
        for w in range(DN_CONV):
            t = cw_ref[w:w + 1, :] * xp_ref[r0 + shift + w:r0 + shift + w + prep, :]
            acc = t if acc is None else acc + t
        y = _silu(acc)
        qq, kk, vv = y[:, :C_WIDTH], y[:, C_WIDTH:2 * C_WIDTH], y[:, 2 * C_WIDTH:]
        rows = slice(r0, r0 + prep)
        q_ref[rows, :] = qq * lax.rsqrt(_head_sumsq(qq, ones_bd) + EPS) * (HEAD_DIM ** -0.5)
        k_ref[rows, :] = kk * lax.rsqrt(_head_sumsq(kk, ones_bd) + EPS)
        v_ref[rows, :] = vv


def _dn_scan_kernel(q_ref, k_ref, v_ref, gb_ref, oc_ref, srec_ref, *, seq, group):
    def run_chunk(r0, chunk, states):
        rows = [pl.ds(b * seq + r0, chunk) for b in range(group)]
        stack = lambda ref: jnp.stack([ref[r, :] for r in rows])
        o, states = _delta_chunk(stack(q_ref), stack(k_ref), stack(v_ref), stack(gb_ref), states, chunk)
        for b, r in enumerate(rows):
            oc_ref[r, :] = o[b]
        return states

    states = run_chunk(0, NUM_META, jnp.zeros((group, C_WIDTH, C_WIDTH), F32))

    def body(i, states):
        return run_chunk(pl.multiple_of(NUM_META + i * DN_CHUNK, SUBLANES), DN_CHUNK, states)

    states = lax.fori_loop(0, (seq - NUM_META) // DN_CHUNK, body, states)
    for b in range(group):
        s_bd = states[b]
        half = s_bd[:, 0:LANES] + s_bd[:, LANES:2 * LANES]
        fold = half + pltpu.roll(half, HEAD_DIM, axis=1)
        srec_ref[b] = fold[:, 0:HEAD_DIM].reshape(C_HEADS, HEAD_DIM, HEAD_DIM)


DN_GROUP = 4


def _deltanet(qkvc, small, cw, alog, dtb, batch, seq):
    n = batch * seq
    vec = _const_spec((1, LANES))
    rows = lambda width: pl.BlockSpec((seq, width), lambda b: (b, 0))
    f = lambda width: jax.ShapeDtypeStruct((n, width), F32)
    q, k, v, gb = pl.pallas_call(
        functools.partial(_dn_prep_kernel, seq=seq),
        grid=(batch,),
        in_specs=[rows(3 * C_WIDTH), rows(LANES), _const_spec(cw.shape), vec, vec],
        out_specs=[rows(C_WIDTH), rows(C_WIDTH), rows(C_WIDTH), rows(LANES)],
        out_shape=[f(C_WIDTH), f(C_WIDTH), f(C_WIDTH), f(LANES)],
        scratch_shapes=[pltpu.VMEM((DN_PAD + seq, 3 * C_WIDTH), F32)],
        compiler_params=_cparams(("parallel",)),
        name="dn_prep",
    )(qkvc, small, cw, alog, dtb)

    group = max(g for g in range(1, DN_GROUP + 1) if batch % g == 0)
    once = lambda width: pl.BlockSpec((group * seq, width), lambda b: (b, 0), pipeline_mode=pl.Buffered(1))
    return pl.pallas_call(
        functools.partial(_dn_scan_kernel, seq=seq, group=group),
        grid=(batch // group,),
        in_specs=[once(C_WIDTH), once(C_WIDTH), once(C_WIDTH), once(LANES)],
        out_specs=[once(C_WIDTH),
                   pl.BlockSpec((group, C_HEADS, HEAD_DIM, HEAD_DIM), lambda b: (b, 0, 0, 0))],
        out_shape=[f(C_WIDTH), jax.ShapeDtypeStruct((batch, C_HEADS, HEAD_DIM, HEAD_DIM), F32)],
        compiler_params=_cparams(("parallel",)),
        name="dn_scan",
    )(q, k, v, gb)


def _out_ffn_kernel(x_ref, oa_ref, ob_ref, oc_ref, z_ref, ga_ref, gb_ref, gc_ref, wo_ref, gf_ref, wg_ref, wu_ref,
                    wd_ref, fin_ref, y_ref, *, final):
    oa = jnp.concatenate([oa_ref[p] for p in range(A_PAIRS)], axis=-1)
    oc = oc_ref[...]
    ms = _head_sumsq(oc, _head_sum_matrix()) * (1.0 / HEAD_DIM)
    oc = oc * lax.rsqrt(ms + EPS) * gc_ref[...] * _silu(z_ref[...])
    mixed = jnp.concatenate([_rms(oa, ga_ref[...]), _rms(ob_ref[...], gb_ref[...]), oc], axis=-1)
    x1 = x_ref[...] + _dot(mixed.astype(BF16), wo_ref[...])
    xn = _rms(x1, gf_ref[...]).astype(BF16)
    hid = _silu(_dot(xn, wg_ref[...])) * _dot(xn, wu_ref[...])
    x2 = x1 + _dot(hid.astype(BF16), wd_ref[...])
    y_ref[...] = _rms(x2, fin_ref[...]) if final else x2


def _out_ffn(x, oa, ob, oc, z, ga, gb, gc, wo, gf, wg, wu, wd, fin, tm, final):
    n, d = x.shape
    row = lambda width: pl.BlockSpec((tm, width), lambda i: (i, 0))
    consts = (ga, gb, gc, wo, gf, wg, wu, wd, fin)
    return pl.pallas_call(
        functools.partial(_out_ffn_kernel, final=final),
        grid=(n // tm,),
        in_specs=[row(d), pl.BlockSpec((A_PAIRS, tm, LANES), lambda i: (0, i, 0)), row(B_WIDTH),
                  row(C_WIDTH), row(C_WIDTH)] + [_const_spec(c.shape) for c in consts],
        out_specs=row(d),
        out_shape=jax.ShapeDtypeStruct((n, d), F32),
        compiler_params=_cparams(("parallel",)),
        name="out_ffn",
    )(x, oa, ob, oc, z, *consts)


POOL_BLOCK = 256
PAGES_PER_STEP = 32


def _pool_suffix_kernel(lf_ref, out_ref):
    pages, heads, rows = lf_ref.shape
    x = lf_ref[...].reshape(pages * heads, rows)
    parts = _split3(x)
    later = (_iota((rows, rows), 0) > _iota((rows, rows), 1)).astype(BF16)
    ones = jnp.ones((rows, rows), BF16)
    excl = tot = None
    for part in parts:
        e, t = _dot(part, later), _dot(part, ones)
        excl = e if excl is None else excl + e
        tot = t if tot is None else tot + t
    out_ref[:, 0:heads, :] = excl.reshape(pages, heads, rows)
    out_ref[:, heads:2 * heads, :] = tot.reshape(pages, heads, rows)


def _pool_suffix(logf_t):
    depth, n_pool, heads, rows = logf_t.shape
    return pl.pallas_call(
        _pool_suffix_kernel,
        grid=(depth, pl.cdiv(n_pool, POOL_BLOCK)),
        in_specs=[pl.BlockSpec((None, POOL_BLOCK, heads, rows), lambda l, i: (l, i, 0, 0))],
        out_specs=pl.BlockSpec((None, POOL_BLOCK, 2 * heads, rows), lambda l, i: (l, i, 0, 0)),
        out_shape=jax.ShapeDtypeStruct((depth, n_pool, 2 * heads, rows), F32),
        compiler_params=_cparams(("parallel", "parallel")),
        name="pool_suffix",
    )(logf_t)


def _fox_decode_kernel(pt_ref, q_ref, kn_ref, vn_ref, cn_ref, *rest):
    pps = PAGES_PER_STEP
    k_refs, v_refs, r_refs = rest[0:pps], rest[pps:2 * pps], rest[2 * pps:3 * pps]
    o_ref, m_ref, l_ref, acc_ref, suf_ref = rest[3 * pps:]
    j = pl.program_id(1)
    own = (_iota((A_HEADS, A_WIDTH), 1) // HEAD_DIM) == _iota((A_HEADS, A_WIDTH), 0)
    qbd = jnp.where(own, q_ref[...], 0.0)

    @pl.when(j == 0)
    def _():
        m_ref[...] = jnp.broadcast_to(jnp.sum(qbd * kn_ref[...], axis=1, keepdims=True), m_ref.shape)
        l_ref[...] = jnp.ones(l_ref.shape, F32)
        acc_ref[...] = jnp.broadcast_to(vn_ref[...], acc_ref.shape)
        suf_ref[...] = jnp.zeros(suf_ref.shape, F32)

    qb = qbd.astype(BF16)
    cn = cn_ref[...]
    m, l, acc, suf = m_ref[...], l_ref[...], acc_ref[...], suf_ref[...]

    def page_t(ref):
        return ref[...].reshape(A_WIDTH, ref.shape[-1]).astype(BF16)

    scores = []
    for u in range(pps):
        r = r_refs[u][...]
        s = (cn + suf + r[0:A_HEADS]) * LOG2E + _dot(qb, page_t(k_refs[u]))
        suf = suf + r[A_HEADS:2 * A_HEADS]
        scores.append(s)
    smax = scores[0]
    for s in scores[1:]:
        smax = jnp.maximum(smax, s)
    m_new = jnp.maximum(m, jnp.max(smax, axis=1, keepdims=True))
    a = jnp.exp2(m - m_new)
    probs = [jnp.exp2(s - m_new) for s in scores]
    psum = probs[0]
    for p in probs[1:]:
        psum = psum + p
    l = a * l + jnp.sum(psum, axis=1, keepdims=True)
    acc = acc * a[:, 0:1]
    for u in range(pps):
        acc = acc + _dot_nt(probs[u].astype(BF16), page_t(v_refs[u]))
    m_ref[...], l_ref[...], acc_ref[...], suf_ref[...] = m_new, l, acc, suf

    @pl.when(j == pl.num_programs(1) - 1)
    def _():
        o_ref[...] = jnp.sum(jnp.where(own, acc / l[:, 0:1], 0.0), axis=0, keepdims=True)


def _fox_decode(page_table, layer, q, k_new, v_new, logf_new, cache_kt, cache_vt, suffix):
    db, n_pages = page_table.shape
    pps = PAGES_PER_STEP
    assert n_pages % pps == 0
    steps = n_pages // pps
    rows = cache_kt.shape[-1]
    tok = pl.BlockSpec((None, 1, A_WIDTH), lambda b, j, pt: (b, 0, 0))

    def page(u, shape):
        zeros = (0,) * len(shape)
        return pl.BlockSpec((None, None) + shape,
                            lambda b, j, pt: (layer, pt[b, n_pages - 1 - (j * pps + u)]) + zeros)

    grid_spec = pltpu.PrefetchScalarGridSpec(
        num_scalar_prefetch=1,
        grid=(db, steps),
        in_specs=[tok, tok, tok, pl.BlockSpec((None, A_HEADS, LANES), lambda b, j, pt: (b, 0, 0))]
                 + [page(u, (A_HEADS, HEAD_DIM, rows)) for u in range(pps)]
                 + [page(u, (A_HEADS, HEAD_DIM, rows)) for u in range(pps)]
                 + [page(u, (2 * A_HEADS, rows)) for u in range(pps)],
        out_specs=tok,
        scratch_shapes=[pltpu.VMEM((A_HEADS, LANES), F32), pltpu.VMEM((A_HEADS, LANES), F32),
                        pltpu.VMEM((A_HEADS, A_WIDTH), F32), pltpu.VMEM((A_HEADS, LANES), F32)],
    )
    tok3 = lambda t: t.reshape(db, 1, A_WIDTH)
    return pl.pallas_call(
        _fox_decode_kernel,
        grid_spec=grid_spec,
        out_shape=jax.ShapeDtypeStruct((db, 1, A_WIDTH), F32),
        compiler_params=_cparams(("parallel", "arbitrary")),
        name="fox_decode",
    )(page_table, tok3(q), tok3(k_new), tok3(v_new), logf_new, *([cache_kt] * pps), *([cache_vt] * pps),
      *([suffix] * pps))


def _sample_mix_kernel(h_ref, cbuf_ref, dw_ref, dwb_ref, lng_ref, lnb_ref, pw_ref, pwb_ref,
                       x_ref, dbuf_ref, cw_ref, small_ref, alog_ref, dtb_ref,
                       ob_ref, q_ref, k_ref, v_ref, eg_ref, bt_ref, lf_ref):
    keep = CONF_WIDTH - 1
    h = h_ref[...]
    cbuf = cbuf_ref[...]
    acc = jnp.sum(cbuf * dw_ref[0:keep, :][None], axis=1) + dw_ref[keep:keep + 1, :] * h + dwb_ref[...]
    mu = jnp.mean(acc, axis=-1, keepdims=True)
    xc = acc - mu
    var = jnp.mean(xc * xc, axis=-1, keepdims=True)
    y = xc * lax.rsqrt(var + EPS) * lng_ref[...] + lnb_ref[...]
    ob_ref[...] = _dot(_silu(y).astype(BF16), pw_ref[...]) + pwb_ref[...]

    dk = DN_CONV - 1
    x = x_ref[...]
    dbuf = dbuf_ref[...]
    conv = jnp.sum(dbuf * cw_ref[0:dk, :][None], axis=1) + cw_ref[dk:dk + 1, :] * x
    yv = _silu(conv)
    qq, kk, vv = yv[:, :C_WIDTH], yv[:, C_WIDTH:2 * C_WIDTH], yv[:, 2 * C_WIDTH:]
    ones_bd = _head_sum_matrix()
    q_ref[...] = qq * lax.rsqrt(_head_sumsq(qq, ones_bd) + EPS) * (HEAD_DIM ** -0.5)
    k_ref[...] = kk * lax.rsqrt(_head_sumsq(kk, ones_bd) + EPS)
    v_ref[...] = vv
    sm = small_ref[...]
    eg_ref[...] = jnp.exp(-jnp.exp(alog_ref[...]) * _softplus(sm + dtb_ref[...]))
    bt_ref[...] = _sigmoid(sm)
    lf_ref[...] = _log_sigmoid(sm)


def _sample_mix(h, cbuf, dw, dwb, lng, lnb, pw, pwb, x, dbuf, cw, small, alog, dtb):
    db = h.shape[0]
    f = lambda *shape: jax.ShapeDtypeStruct(shape, F32)
    return pl.pallas_call(
        _sample_mix_kernel,
        out_shape=[f(db, B_WIDTH), f(db, C_WIDTH), f(db, C_WIDTH), f(db, C_WIDTH),
                   f(db, LANES), f(db, LANES), f(db, LANES)],
        compiler_params=pltpu.CompilerParams(vmem_limit_bytes=VMEM_LIMIT_BYTES),
        name="sample_mix",
    )(h, cbuf, dw, dwb, lng, lnb, pw, pwb, x, dbuf, cw, small, alog, dtb)


def _sample_delta_kernel(s_ref, qc_ref, kc_ref, qr_ref, kr_ref, v_ref, eg_ref, bt_ref, o_ref, snew_ref):
    sd = s_ref[...] * eg_ref[...]
    ks = jnp.sum(kc_ref[...] * sd, axis=1, keepdims=True)
    u = bt_ref[...] * (v_ref[...] - ks)
    qs = jnp.sum(qc_ref[...] * sd, axis=1, keepdims=True)
    qk = jnp.sum(qr_ref[...] * kr_ref[...], axis=2, keepdims=True)
    o_ref[...] = qs + qk * u
    snew_ref[...] = sd + kc_ref[...] * u


def _sample_delta(s, qn, kn, vn, eg, bt):
    db = s.shape[0]
    g = db * C_HEADS
    col = lambda t: t.reshape(g, HEAD_DIM, 1)
    row = lambda t: t.reshape(g, 1, HEAD_DIM)
    gate = lambda t, off: t[:, off:off + C_HEADS].reshape(g, 1, 1)
    o, s_new = pl.pallas_call(
        _sample_delta_kernel,
        out_shape=[jax.ShapeDtypeStruct((g, 1, HEAD_DIM), F32), jax.ShapeDtypeStruct((g, HEAD_DIM, HEAD_DIM), F32)],
        compiler_params=pltpu.CompilerParams(vmem_limit_bytes=VMEM_LIMIT_BYTES),
        name="sample_delta",
    )(s.reshape(g, HEAD_DIM, HEAD_DIM), col(qn), col(kn), row(qn), row(kn), row(vn), gate(eg, SM_A), gate(bt, SM_B))
    return o.reshape(db, C_WIDTH), s_new.reshape(db, C_HEADS, HEAD_DIM, HEAD_DIM)


def _permute_in_cols(t):
    aw, bw, cw = A_WIDTH, B_WIDTH, C_WIDTH
    f0 = 3 * aw
    glu0 = f0 + A_HEADS
    qkv0 = glu0 + 2 * bw
    a0 = qkv0 + 3 * cw
    b0 = a0 + C_HEADS
    z0 = b0 + C_HEADS
    parts = [t[..., 0:f0], t[..., glu0:qkv0], t[..., qkv0:a0], t[..., z0:z0 + cw],
             t[..., f0:glu0], t[..., a0:b0], t[..., b0:z0]]
    out = jnp.concatenate(parts, axis=-1)
    return jnp.pad(out, [(0, 0)] * (t.ndim - 1) + [(0, IN_COLS_PAD - out.shape[-1])])


def _lane_vec(vals, off):
    return jnp.zeros((1, LANES), F32).at[0, off:off + vals.shape[0]].set(vals)


def _row_tile(n):
    for tm in (384, 256, 128, 64, 32, 16, 8):
        if n % tm == 0:
            return tm
    raise ValueError(n)


def kernel(x_prompt, x_sample, cache_k, cache_v, cache_logf, state_conf_conv, state_dn_conv, state_dn_rec, page_table, meta_tokens, norm_mix, w_in, b_in, conf_dw, conf_dw_b, conf_ln_g, conf_ln_b, conf_pw, conf_pw_b, dn_conv, dn_a_log, dn_dt_bias, dn_norm_g, grp_norm_a, grp_norm_b, w_out, norm_ffn, w_ffn_gate, w_ffn_up, w_ffn_down, final_norm):
    batch, seq_in, d = x_prompt.shape
    db = x_sample.shape[0]
    assert x_sample.shape[1] == 1
    depth = w_in.shape[0]
    seq = NUM_META + seq_in
    n = batch * seq

    xp = jnp.concatenate([jnp.broadcast_to(meta_tokens[None], (batch, NUM_META, d)), x_prompt], axis=1).reshape(n, d)
    xs = x_sample.reshape(db, d)
    w_in_p = _permute_in_cols(w_in).astype(BF16)
    b_in_p = _permute_in_cols(b_in)
    tm_p, tm_s = _row_tile(n), _row_tile(db)
    fin = final_norm.reshape(1, d)
    r1 = lambda t: t.reshape(1, -1)
    suffix = _pool_suffix(jnp.transpose(cache_logf, (0, 1, 3, 2)))
    cache_kt = jnp.transpose(cache_k, (0, 1, 3, 4, 2))
    cache_vt = jnp.transpose(cache_v, (0, 1, 3, 4, 2))

    p_out = [[] for _ in range(6)]
    s_out = [[] for _ in range(6)]
    for l in range(depth):
        g_mix, g_ffn = r1(norm_mix[l]), r1(norm_ffn[l])
        wo, wg, wu, wd = (w_out[l].astype(BF16), w_ffn_gate[l].astype(BF16), w_ffn_up[l].astype(BF16),
                          w_ffn_down[l].astype(BF16))
        pw = conf_pw[l].astype(BF16)
        alog, dtb = _lane_vec(dn_a_log[l], SM_A), _lane_vec(dn_dt_bias[l], SM_A)
        ng_t = jnp.tile(dn_norm_g[l], C_HEADS).reshape(1, C_WIDTH)
        last = l == depth - 1
        conf_args = (conf_dw[l], r1(conf_dw_b[l]), r1(conf_ln_g[l]), r1(conf_ln_b[l]), pw, r1(conf_pw_b[l]))
        ffn_args = (r1(grp_norm_a[l]), r1(grp_norm_b[l]), ng_t, wo, g_ffn, wg, wu, wd, fin)

        qb, kb, vb, _, kf, vf, h, qkvc, z, small = _inproj(xp, g_mix, w_in_p[l], b_in_p[l:l + 1], tm_p)
        logf, qaug, kaug = _fox_gates(small, batch, seq)
        oa = _fox_prompt(qb, kb, vb, qaug, kaug, batch, seq)
        ob = _conformer(h, *conf_args, batch, seq)
        conf_buf = h.reshape(batch, seq, B_WIDTH)[:, seq - (CONF_WIDTH - 1):]
        dn_buf = qkvc.reshape(batch, seq, 3 * C_WIDTH)[:, seq - (DN_CONV - 1):]
        oc, s_rec = _deltanet(qkvc, small, dn_conv[l], alog, dtb, batch, seq)
        xp = _out_ffn(xp, oa, ob, oc, z, *ffn_args, tm_p, last)
        for i, t in enumerate((kf.reshape(batch, seq, A_HEADS, HEAD_DIM), vf.reshape(batch, seq, A_HEADS, HEAD_DIM),
                               logf, conf_buf, dn_buf, s_rec)):
            p_out[i].append(t)

        _, _, _, qf, kf, vf, h, qkvc, z, small = _inproj(xs, g_mix, w_in_p[l], b_in_p[l:l + 1], tm_s)
        conf_new = jnp.concatenate([state_conf_conv[l][:, 1:], h[:, None]], axis=1)
        dn_new = jnp.concatenate([state_dn_conv[l][:, 1:], qkvc[:, None]], axis=1)
        ob, qn, kn, vn, eg, bt, lf = _sample_mix(
            h, state_conf_conv[l], *conf_args, qkvc, state_dn_conv[l], dn_conv[l], small, alog, dtb)
        lf_heads = lf[:, SM_F:SM_F + A_HEADS]
        cn = jnp.broadcast_to(lf_heads[:, :, None], (db, A_HEADS, LANES))
        oa = _fox_decode(page_table, l, qf, kf, vf, cn, cache_kt, cache_vt, suffix)
        oa = jnp.transpose(oa.reshape(db, A_PAIRS, LANES), (1, 0, 2))
        oc, s_new = _sample_delta(state_dn_rec[l], qn, kn, vn, eg, bt)
        xs = _out_ffn(xs, oa, ob, oc, z, *ffn_args, tm_s, last)
        for i, t in enumerate((kf.reshape(db, 1, A_HEADS, HEAD_DIM), vf.reshape(db, 1, A_HEADS, HEAD_DIM),
                               lf_heads.reshape(db, 1, A_HEADS), conf_new, dn_new, s_new)):
            s_out[i].append(t)

    y_prompt = xp.reshape(batch, seq, d)[:, NUM_META:]
    y_sample = xs.reshape(db, 1, d)
    return (y_prompt, y_sample, *[jnp.stack(a) for a in p_out], *[jnp.stack(a) for a in s_out])
```

```python
import functools

import jax
import jax.numpy as jnp
from jax import lax
from jax.experimental import pallas as pl
from jax.experimental.pallas import tpu as pltpu

F32 = jnp.float32
BF16 = jnp.bfloat16
HIGHEST = lax.Precision.HIGHEST

LANES = 128
SUBLANES = 8
VMEM_LIMIT_BYTES = 56 * 1024 * 1024

NUM_META = 16
A_HEADS = 8
HEAD_DIM = 64
A_WIDTH = A_HEADS * HEAD_DIM
A_PAIRS = A_WIDTH // LANES
B_WIDTH = 256
CONF_WIDTH = 31
C_HEADS = 4
C_WIDTH = C_HEADS * HEAD_DIM
DN_CONV = 4
DN_CHUNK = 64
EPS = 1e-6
NEG_BIG = -1e30
LOG2E = 1.4426950408889634

_Q0, _K0, _V0, _GLU0, _QKVC0, _Z0, _SM0 = 0, 512, 1024, 1536, 2048, 2816, 3072
IN_COLS_PAD = 3200
SM_F, SM_A, SM_B = 0, 8, 12


def _cparams(sem):
    return pltpu.CompilerParams(dimension_semantics=sem, vmem_limit_bytes=VMEM_LIMIT_BYTES)


def _const_spec(shape):
    nd = len(shape)
    return pl.BlockSpec(shape, lambda *_: (0,) * nd, pipeline_mode=pl.Buffered(1))


def _batched(a, b):
    if a.ndim == 2:
        a = jnp.broadcast_to(a, b.shape[:1] + a.shape)
    if b.ndim == 2:
        b = jnp.broadcast_to(b, a.shape[:1] + b.shape)
    return a, b


def _dot(a, b, precision=None):
    if a.ndim == 3 or b.ndim == 3:
        a, b = _batched(a, b)
        return lax.dot_general(a, b, (((2,), (1,)), ((0,), (0,))), preferred_element_type=F32, precision=precision)
    return jnp.dot(a, b, preferred_element_type=F32, precision=precision)


def _dot_nt(a, b, precision=None):
    if a.ndim == 3 or b.ndim == 3:
        a, b = _batched(a, b)
        return lax.dot_general(a, b, (((2,), (2,)), ((0,), (0,))), preferred_element_type=F32, precision=precision)
    return lax.dot_general(a, b, (((1,), (1,)), ((), ())), preferred_element_type=F32, precision=precision)


def _sigmoid(x):
    return 1.0 / (1.0 + jnp.exp(-x))


def _silu(x):
    return x * _sigmoid(x)


def _log_sigmoid(x):
    return jnp.minimum(x, 0.0) - jnp.log1p(jnp.exp(-jnp.abs(x)))


def _softplus(x):
    return jnp.maximum(x, 0.0) + jnp.log1p(jnp.exp(-jnp.abs(x)))


def _rms(x, g):
    return x * lax.rsqrt(jnp.mean(x * x, axis=-1, keepdims=True) + EPS) * g


def _iota(shape, dim):
    return lax.broadcasted_iota(jnp.int32, shape, dim)


def _split2(x):
    hi = x.astype(BF16)
    return hi, (x - hi.astype(F32)).astype(BF16)


def _split3(x):
    hi = x.astype(BF16)
    r = x - hi.astype(F32)
    mid = r.astype(BF16)
    return hi, mid, (r - mid.astype(F32)).astype(BF16)


def _dot_split(a, b):
    (ah, al), (bh, bl) = a, b
    return _dot(ah, bh) + (_dot(ah, bl) + _dot(al, bh))


def _lane_col(x, lane):
    return jnp.sum(jnp.where(_iota((1, x.shape[-1]), 1) == lane, x, 0.0), axis=-1, keepdims=True)


def _inproj_kernel(x_ref, g_ref, w_ref, b_ref, qb_ref, kb_ref, vb_ref, qf_ref, kf_ref, vf_ref,
                   h_ref, qkvc_ref, z_ref, small_ref, *, kv_minor_rows):
    xn = _rms(x_ref[...], g_ref[...]).astype(BF16)

    def seg(lo, hi):
        return _dot(xn, w_ref[:, lo:hi]) + b_ref[:, lo:hi]

    q = seg(_Q0, _K0) * (HEAD_DIM ** -0.5 * LOG2E)
    k = seg(_K0, _V0)
    v = seg(_V0, _GLU0)
    qf_ref[...] = q
    kf_ref[...] = k.T if kv_minor_rows else k
    vf_ref[...] = v.T if kv_minor_rows else v
    for p in range(A_PAIRS):
        sl = slice(p * LANES, (p + 1) * LANES)
        qb_ref[p] = q[:, sl].astype(BF16)
        kb_ref[p] = k[:, sl].astype(BF16)
        vb_ref[p] = v[:, sl].astype(BF16)
    glu = seg(_GLU0, _QKVC0)
    h_ref[...] = glu[:, :B_WIDTH] * _sigmoid(glu[:, B_WIDTH:])
    qkvc_ref[...] = seg(_QKVC0, _Z0)
    z_ref[...] = seg(_Z0, _SM0)
    small_ref[...] = seg(_SM0, IN_COLS_PAD)


def _inproj(x, g, w, b, tm, kv_minor_rows):
    n, d = x.shape
    row = lambda width: pl.BlockSpec((tm, width), lambda i: (i, 0))
    pair = pl.BlockSpec((A_PAIRS, tm, LANES), lambda i: (0, i, 0))
    pair_shape = jax.ShapeDtypeStruct((A_PAIRS, n, LANES), BF16)
    f = lambda width: jax.ShapeDtypeStruct((n, width), F32)
    kv = pl.BlockSpec((A_WIDTH, tm), lambda i: (0, i)) if kv_minor_rows else row(A_WIDTH)
    kv_shape = jax.ShapeDtypeStruct((A_WIDTH, n), F32) if kv_minor_rows else f(A_WIDTH)
    return pl.pallas_call(
        functools.partial(_inproj_kernel, kv_minor_rows=kv_minor_rows),
        grid=(n // tm,),
        in_specs=[row(d), _const_spec((1, d)), _const_spec(w.shape), _const_spec(b.shape)],
        out_specs=[pair, pair, pair, row(A_WIDTH), kv, kv, row(B_WIDTH),
                   row(3 * C_WIDTH), row(C_WIDTH), row(LANES)],
        out_shape=[pair_shape, pair_shape, pair_shape, f(A_WIDTH), kv_shape, kv_shape, f(B_WIDTH),
                   f(3 * C_WIDTH), f(C_WIDTH), f(LANES)],
        compiler_params=_cparams(("parallel",)),
        name="inproj",
    )(x, g, w, b)


AUG_PARTS = 3
AUG_HEAD = 2 * AUG_PARTS


def _fox_gates_kernel(small_ref, logf_ref, qaug_ref, kaug_ref, c_ref, *, seq):
    lf = _log_sigmoid(small_ref[...])
    logf_ref[...] = lf[:, SM_F:SM_F + A_HEADS]
    tril = (_iota((LANES, LANES), 0) >= _iota((LANES, LANES), 1)).astype(F32)
    carry = jnp.zeros((1, LANES), F32)
    full = seq // LANES
    for j in range(full):
        cs = _dot(tril, lf[j * LANES:(j + 1) * LANES], HIGHEST) + carry
        c_ref[j * LANES:(j + 1) * LANES, :] = cs
        carry = cs[LANES - 1:LANES]
    tail = seq - full * LANES
    if tail:
        c_ref[full * LANES:seq, :] = _dot(tril[:tail, :tail], lf[full * LANES:seq], HIGHEST) + carry

    parts = _split3(c_ref[...] * LOG2E)
    row, col = _iota((LANES, LANES), 0), _iota((LANES, LANES), 1)
    p3 = None
    for i, part in enumerate(parts):
        t = _dot(part, ((col == AUG_PARTS * row + i) & (row < A_HEADS)).astype(BF16))
        p3 = t if p3 is None else p3 + t
    p3 = p3.astype(BF16)
    lane = _iota((1, LANES), 1)
    slot = lane // AUG_PARTS
    for p in range(A_PAIRS):
        src = AUG_HEAD * p + (col // AUG_HEAD) * AUG_PARTS + col % AUG_PARTS
        hit = row == src
        cslot = col // AUG_PARTS
        gq = (hit & ((cslot == 0) | (cslot == 2))).astype(BF16)
        gk = (hit & ((cslot == 1) | (cslot == 3))).astype(BF16)
        one_q = ((slot == 1) | (slot == 3)).astype(F32)
        one_k = ((slot == 0) | (slot == 2)).astype(F32)
        qaug_ref[p] = (_dot(p3, gq) + one_q).astype(BF16)
        kaug_ref[p] = (one_k - _dot(p3, gk)).astype(BF16)


def _fox_gates(small, batch, seq):
    pair = pl.BlockSpec((A_PAIRS, seq, LANES), lambda b: (0, b, 0))
    pair_shape = jax.ShapeDtypeStruct((A_PAIRS, batch * seq, LANES), BF16)
    return pl.pallas_call(
        functools.partial(_fox_gates_kernel, seq=seq),
        grid=(batch,),
        in_specs=[pl.BlockSpec((seq, LANES), lambda b: (b, 0))],
        out_specs=[pl.BlockSpec((None, seq, A_HEADS), lambda b: (b, 0, 0)), pair, pair],
        out_shape=[jax.ShapeDtypeStruct((batch, seq, A_HEADS), F32), pair_shape, pair_shape],
        scratch_shapes=[pltpu.VMEM((seq, LANES), F32)],
        compiler_params=_cparams(("parallel",)),
        name="fox_gates",
    )(small)


Q_ROWS = 256


def _fox_prompt_kernel(q_ref, k_ref, v_ref, qa_ref, ka_ref, o_ref, kcat_ref, v0_ref, v1_ref, *, seq):
    lane = _iota((1, LANES), 1)
    lo = lane < HEAD_DIM
    zero = jnp.zeros((), BF16)
    kcat_ref[:, 0:LANES] = k_ref[...]
    kcat_ref[:, LANES:2 * LANES] = ka_ref[...]
    v = v_ref[...]
    v0_ref[...] = jnp.where(lo, v, zero)
    v1_ref[...] = jnp.where(lo, zero, v)
    heads = ((lo, lane < AUG_HEAD, v0_ref), (~lo, (lane >= AUG_HEAD) & (lane < 2 * AUG_HEAD), v1_ref))

    def block(r0, nq):
        q, qa = q_ref[r0:r0 + nq, :], qa_ref[r0:r0 + nq, :]
        causal = _iota((nq, nq), 0) >= _iota((nq, nq), 1)
        out = None
        for lanes_h, aug_h, vh_ref in heads:
            qc = jnp.concatenate([jnp.where(lanes_h, q, zero), jnp.where(aug_h, qa, zero)], axis=1)
            s_d = jnp.where(causal, _dot_nt(qc, kcat_ref[r0:r0 + nq, :]), NEG_BIG)
            m = jnp.max(s_d, axis=1, keepdims=True)
            if r0:
                s_f = _dot_nt(qc, kcat_ref[0:r0, :])
                m = jnp.maximum(m, jnp.max(s_f, axis=1, keepdims=True))
            p_d = jnp.exp2(s_d - m)
            l = jnp.sum(p_d, axis=1, keepdims=True)
            o = _dot(p_d.astype(BF16), vh_ref[r0:r0 + nq, :])
            if r0:
                p_f = jnp.exp2(s_f - m)
                l = l + jnp.sum(p_f, axis=1, keepdims=True)
                o = o + _dot(p_f.astype(BF16), vh_ref[0:r0, :])
            o = o * (1.0 / l)
            out = o if out is None else out + o
        o_ref[r0:r0 + nq, :] = out

    full = seq // Q_ROWS
    for i in range(full):
        block(i * Q_ROWS, Q_ROWS)
    if seq - full * Q_ROWS:
        block(full * Q_ROWS, seq - full * Q_ROWS)


def _fox_prompt(qb, kb, vb, qaug, kaug, batch, seq):
    n = batch * seq
    tile = pl.BlockSpec((None, seq, LANES), lambda b, p: (p, b, 0))
    return pl.pallas_call(
        functools.partial(_fox_prompt_kernel, seq=seq),
        grid=(batch, A_PAIRS),
        in_specs=[tile] * 5,
        out_specs=tile,
        out_shape=jax.ShapeDtypeStruct((A_PAIRS, n, LANES), F32),
        scratch_shapes=[pltpu.VMEM((seq, 2 * LANES), BF16), pltpu.VMEM((seq, LANES), BF16),
                        pltpu.VMEM((seq, LANES), BF16)],
        compiler_params=_cparams(("parallel", "parallel")),
        name="fox_prompt",
    )(qb, kb, vb, qaug, kaug)


CONF_ROWS = 48
CONF_PAD = 32


def _conformer_kernel(h_ref, dw_ref, dwb_ref, lng_ref, lnb_ref, pw_ref, pwb_ref, ob_ref, hp_ref, *, seq):
    hp_ref[0:CONF_PAD, :] = jnp.zeros((CONF_PAD, B_WIDTH), F32)
    hp_ref[CONF_PAD:CONF_PAD + seq, :] = h_ref[...]
    shift = CONF_PAD - (CONF_WIDTH - 1)

    def chunk(c):
        r0 = c * CONF_ROWS
        if not isinstance(r0, int):
            r0 = pl.multiple_of(r0, SUBLANES)
        win = hp_ref[pl.ds(r0, CONF_ROWS + CONF_PAD), :]
        rows = CONF_ROWS + CONF_PAD
        phases = [win] + [pltpu.roll(win, rows - r, axis=0) for r in range(1, SUBLANES)]
        acc = jnp.zeros((CONF_ROWS, B_WIDTH), F32) + dwb_ref[...]
        for w in range(CONF_WIDTH):
            base, r = (shift + w) // SUBLANES * SUBLANES, (shift + w) % SUBLANES
            acc = acc + dw_ref[w:w + 1, :] * phases[r][base:base + CONF_ROWS, :]
        mu = jnp.mean(acc, axis=-1, keepdims=True)
        xc = acc - mu
        var = jnp.mean(xc * xc, axis=-1, keepdims=True)
        y = xc * lax.rsqrt(var + EPS) * lng_ref[...] + lnb_ref[...]
        ob_ref[pl.ds(r0, CONF_ROWS), :] = _dot(_silu(y).astype(BF16), pw_ref[...]) + pwb_ref[...]

    def pair(i, carry):
        chunk(2 * i)
        chunk(2 * i + 1)
        return carry

    n_chunks = seq // CONF_ROWS
    lax.fori_loop(0, n_chunks // 2, pair, 0)
    if n_chunks % 2:
        chunk(n_chunks - 1)


def _conformer(h, dw, dwb, lng, lnb, pw, pwb, batch, seq):
    assert seq % CONF_ROWS == 0
    vec = _const_spec((1, B_WIDTH))
    return pl.pallas_call(
        functools.partial(_conformer_kernel, seq=seq),
        grid=(batch,),
        in_specs=[pl.BlockSpec((seq, B_WIDTH), lambda b: (b, 0)), _const_spec(dw.shape), vec, vec, vec,
                  _const_spec(pw.shape), vec],
        out_specs=pl.BlockSpec((seq, B_WIDTH), lambda b: (b, 0)),
        out_shape=jax.ShapeDtypeStruct((batch * seq, B_WIDTH), F32),
        scratch_shapes=[pltpu.VMEM((CONF_PAD + seq, B_WIDTH), F32)],
        compiler_params=_cparams(("parallel",)),
        name="conformer",
    )(h, dw, dwb, lng, lnb, pw, pwb)


DN_PAD = 8
DN_PREP_ROWS = 344


def _divisor_rows(seq, cap):
    return max(r for r in range(SUBLANES, cap + 1, SUBLANES) if seq % r == 0)


def _head_sum_matrix():
    return ((_iota((C_WIDTH, C_WIDTH), 0) // HEAD_DIM) == (_iota((C_WIDTH, C_WIDTH), 1) // HEAD_DIM)).astype(BF16)


def _head_sumsq(x, ones_bd):
    hi, lo = _split2(x * x)
    return _dot(hi, ones_bd) + _dot(lo, ones_bd)


def _unit_lower_inverse(ns, size):
    n = ns.shape[-1]
    eye = (_iota((n, n), 0) == _iota((n, n), 1)).astype(F32)
    inv = eye - ns
    pw = ns.astype(BF16)
    k = 2
    while k < size:
        pw = _dot(pw, pw).astype(BF16)
        inv = inv + _dot(inv.astype(BF16), pw)
        k *= 2
    resid = (eye - inv) - _dot_split(_split2(ns), _split2(inv))
    return _split2(inv + _dot(inv.astype(BF16), resid.astype(BF16)))


def _expand_heads(x):
    head = _iota((1, x.shape[-1]), 1) // HEAD_DIM
    return jnp.concatenate([jnp.where(head == h, x, 0.0) for h in range(x.shape[-1] // HEAD_DIM)], axis=-2)


def _delta_chunk(q, k, v, gb, s_bd, chunk):
    g = bt = gb
    c = chunk
    width = q.shape[-1]
    heads = width // HEAD_DIM
    n = heads * c
    lead = q.shape[:-2]
    tril_c = (_iota((c, c), 0) >= _iota((c, c), 1)).astype(F32)
    gcum = _dot(tril_c, g, HIGHEST)
    last = gcum[..., c - 1:c, :]
    rows_cat = lambda parts: jnp.concatenate(parts, axis=-2)
    gcol = rows_cat([_lane_col(gcum, SM_A + h) for h in range(heads)])
    bcol = rows_cat([_lane_col(bt, SM_B + h) for h in range(heads)])
    glast = rows_cat([jnp.broadcast_to(_lane_col(last, SM_A + h), lead + (c, 1)) for h in range(heads)])
    gmask = rows_cat([jnp.where(_iota((1, LANES), 1) == SM_A + h, gcum, 0.0) for h in range(heads)])
    ones_b = jnp.ones((n, LANES), BF16)
    grow = None
    for part in _split3(gmask):
        t = _dot_nt(ones_b, part)
        grow = t if grow is None else grow + t
    ri, ci = _iota((n, n), 0), _iota((n, n), 1)
    same = (ri // c) == (ci // c)
    tri = same & (ri >= ci)
    strict = same & (ri > ci)
    decay = jnp.where(tri, jnp.exp(jnp.where(tri, gcol - grow, 0.0)), 0.0)

    kx, qx, vx = _expand_heads(k), _expand_heads(q), _expand_heads(v)
    kb = kx * bcol
    kxb = kx.astype(BF16)
    a_strict = jnp.where(strict, _dot_nt(kb.astype(BF16), kxb) * decay, 0.0)
    inv = _unit_lower_inverse(a_strict, c)
    eg = jnp.exp(gcol)
    sol = _dot_split(inv, _split2(jnp.concatenate([vx * bcol, kb * eg], axis=-1)))
    u0, w = sol[..., :width], sol[..., width:]
    qk = _dot_nt(qx.astype(BF16), kxb) * decay
    q_dec = qx * eg
    k_dec = kx * jnp.exp(glast - gcol)
    sb = s_bd.astype(BF16)
    u = u0 - _dot(w.astype(BF16), sb)
    ub = u.astype(BF16)
    o = _dot(q_dec.astype(BF16), sb) + _dot(qk.astype(BF16), ub)
    gl_rows = rows_cat([jnp.broadcast_to(jnp.exp(_lane_col(last, SM_A + h)), lead + (HEAD_DIM, 1))
                        for h in range(heads)])
    s_new = s_bd * gl_rows + _dot(jnp.swapaxes(k_dec, -1, -2).astype(BF16), ub)
    o_c = o[..., 0:c, :]
    for h in range(1, heads):
        o_c = o_c + o[..., h * c:(h + 1) * c, :]
    return o_c, s_new


def _dn_prep_kernel(x_ref, small_ref, cw_ref, alog_ref, dtb_ref, q_ref, k_ref, v_ref, gb_ref, xp_ref, *, seq):
    keep = DN_CONV - 1
    xp_ref[0:DN_PAD, :] = jnp.zeros((DN_PAD, 3 * C_WIDTH), F32)
    xp_ref[DN_PAD:DN_PAD + seq, :] = x_ref[...]
    ones_bd = _head_sum_matrix()
    sm = small_ref[...]
    lane = _iota((1, LANES), 1)
    g = -jnp.exp(alog_ref[...]) * _softplus(sm + dtb_ref[...])
    gb_ref[...] = jnp.where((lane >= SM_A) & (lane < SM_A + C_HEADS), g, _sigmoid(sm))

    shift = DN_PAD - keep
    prep = _divisor_rows(seq, DN_PREP_ROWS)
    for c in range(seq // prep):
        r0 = c * prep
        acc = None
        for w in range(DN_CONV):
            t = cw_ref[w:w + 1, :] * xp_ref[r0 + shift + w:r0 + shift + w + prep, :]
            acc = t if acc is None else acc + t
        y = _silu(acc)
        qq, kk, vv = y[:, :C_WIDTH], y[:, C_WIDTH:2 * C_WIDTH], y[:, 2 * C_WIDTH:]
        rows = slice(r0, r0 + prep)
        q_ref[rows, :] = qq * lax.rsqrt(_head_sumsq(qq, ones_bd) + EPS) * (HEAD_DIM ** -0.5)
        k_ref[rows, :] = kk * lax.rsqrt(_head_sumsq(kk, ones_bd) + EPS)
        v_ref[rows, :] = vv


def _dn_scan_kernel(q_ref, k_ref, v_ref, gb_ref, oc_ref, srec_ref, *, seq, group):
    pairs = C_WIDTH // LANES
    pair_heads = LANES // HEAD_DIM
    problems = [(b, p) for b in range(group) for p in range(pairs)]

    def run_chunk(r0, chunk, states):
        rows = [pl.ds(b * seq + r0, chunk) for b in range(group)]
        lanes = [slice(p * LANES, (p + 1) * LANES) for p in range(pairs)]
        stack = lambda ref: jnp.stack([ref[rows[b], lanes[p]] for b, p in problems])

        def gates(b, p):
            gb = gb_ref[rows[b], :]
            return gb if p == 0 else pltpu.roll(gb, LANES - p * pair_heads, axis=1)

        o, states = _delta_chunk(stack(q_ref), stack(k_ref), stack(v_ref),
                                 jnp.stack([gates(b, p) for b, p in problems]), states, chunk)
        for i, (b, p) in enumerate(problems):
            oc_ref[rows[b], lanes[p]] = o[i]
        return states

    states = run_chunk(0, NUM_META, jnp.zeros((len(problems), LANES, LANES), F32))

    def body(i, states):
        return run_chunk(pl.multiple_of(NUM_META + i * DN_CHUNK, SUBLANES), DN_CHUNK, states)

    states = lax.fori_loop(0, (seq - NUM_META) // DN_CHUNK, body, states)
    for i, (b, p) in enumerate(problems):
        s_bd = states[i]
        fold = s_bd + pltpu.roll(s_bd, HEAD_DIM, axis=1)
        srec_ref[b, p * pair_heads:(p + 1) * pair_heads] = fold[:, 0:HEAD_DIM].reshape(pair_heads, HEAD_DIM, HEAD_DIM)


DN_GROUP = 4


def _deltanet(qkvc, small, cw, alog, dtb, batch, seq):
    n = batch * seq
    vec = _const_spec((1, LANES))
    rows = lambda width: pl.BlockSpec((seq, width), lambda b: (b, 0))
    f = lambda width: jax.ShapeDtypeStruct((n, width), F32)
    q, k, v, gb = pl.pallas_call(
        functools.partial(_dn_prep_kernel, seq=seq),
        grid=(batch,),
        in_specs=[rows(3 * C_WIDTH), rows(LANES), _const_spec(cw.shape), vec, vec],
        out_specs=[rows(C_WIDTH), rows(C_WIDTH), rows(C_WIDTH), rows(LANES)],
        out_shape=[f(C_WIDTH), f(C_WIDTH), f(C_WIDTH), f(LANES)],
        scratch_shapes=[pltpu.VMEM((DN_PAD + seq, 3 * C_WIDTH), F32)],
        compiler_params=_cparams(("parallel",)),
        name="dn_prep",
    )(qkvc, small, cw, alog, dtb)

    group = max(g for g in range(1, DN_GROUP + 1) if batch % g == 0)
    once = lambda width: pl.BlockSpec((group * seq, width), lambda b: (b, 0), pipeline_mode=pl.Buffered(1))
    return pl.pallas_call(
        functools.partial(_dn_scan_kernel, seq=seq, group=group),
        grid=(batch // group,),
        in_specs=[once(C_WIDTH), once(C_WIDTH), once(C_WIDTH), once(LANES)],
        out_specs=[once(C_WIDTH),
                   pl.BlockSpec((group, C_HEADS, HEAD_DIM, HEAD_DIM), lambda b: (b, 0, 0, 0))],
        out_shape=[f(C_WIDTH), jax.ShapeDtypeStruct((batch, C_HEADS, HEAD_DIM, HEAD_DIM), F32)],
        compiler_params=_cparams(("parallel",)),
        name="dn_scan",
    )(q, k, v, gb)


def _out_ffn_kernel(x_ref, oa_ref, ob_ref, oc_ref, z_ref, ga_ref, gb_ref, gc_ref, wo_ref, gf_ref, wg_ref, wu_ref,
                    wd_ref, fin_ref, y_ref, *, final):
    oa = jnp.concatenate([oa_ref[p] for p in range(A_PAIRS)], axis=-1)
    oc = oc_ref[...]
    ms = _head_sumsq(oc, _head_sum_matrix()) * (1.0 / HEAD_DIM)
    oc = oc * lax.rsqrt(ms + EPS) * gc_ref[...] * _silu(z_ref[...])
    mixed = jnp.concatenate([_rms(oa, ga_ref[...]), _rms(ob_ref[...], gb_ref[...]), oc], axis=-1)
    x1 = x_ref[...] + _dot(mixed.astype(BF16), wo_ref[...])
    xn = _rms(x1, gf_ref[...]).astype(BF16)
    hid = _silu(_dot(xn, wg_ref[...])) * _dot(xn, wu_ref[...])
    x2 = x1 + _dot(hid.astype(BF16), wd_ref[...])
    y_ref[...] = _rms(x2, fin_ref[...]) if final else x2


def _out_ffn(x, oa, ob, oc, z, ga, gb, gc, wo, gf, wg, wu, wd, fin, tm, final):
    n, d = x.shape
    row = lambda width: pl.BlockSpec((tm, width), lambda i: (i, 0))
    consts = (ga, gb, gc, wo, gf, wg, wu, wd, fin)
    return pl.pallas_call(
        functools.partial(_out_ffn_kernel, final=final),
        grid=(n // tm,),
        in_specs=[row(d), pl.BlockSpec((A_PAIRS, tm, LANES), lambda i: (0, i, 0)), row(B_WIDTH),
                  row(C_WIDTH), row(C_WIDTH)] + [_const_spec(c.shape) for c in consts],
        out_specs=row(d),
        out_shape=jax.ShapeDtypeStruct((n, d), F32),
        compiler_params=_cparams(("parallel",)),
        name="out_ffn",
    )(x, oa, ob, oc, z, *consts)


def _out_ffn_last(x, oa, ob, oc, z, ga, gb, gc, wo, gf, wg, wu, wd, fin, batch, seq):
    d = x.shape[1]
    out_rows = seq - NUM_META
    tm = max(t for t in (512, 256, 128, 64, 32, 16) if out_rows % t == 0)
    start = lambda b, j: pl.multiple_of(b * seq + NUM_META + j * tm, NUM_META)
    row = lambda width: pl.BlockSpec((pl.Element(tm), pl.Element(width)), lambda b, j: (start(b, j), 0))
    pair = pl.BlockSpec((pl.Element(A_PAIRS), pl.Element(tm), pl.Element(LANES)), lambda b, j: (0, start(b, j), 0))
    consts = (ga, gb, gc, wo, gf, wg, wu, wd, fin)
    return pl.pallas_call(
        functools.partial(_out_ffn_kernel, final=True),
        grid=(batch, out_rows // tm),
        in_specs=[row(d), pair, row(B_WIDTH), row(C_WIDTH), row(C_WIDTH)] + [_const_spec(c.shape) for c in consts],
        out_specs=pl.BlockSpec((None, tm, d), lambda b, j: (b, j, 0)),
        out_shape=jax.ShapeDtypeStruct((batch, out_rows, d), F32),
        compiler_params=_cparams(("parallel", "parallel")),
        name="out_ffn_last",
    )(x, oa, ob, oc, z, *consts)


POOL_BLOCK = 256
PAGES_PER_STEP = 32


def _pool_suffix_kernel(lf_ref, out_ref):
    pages, heads, rows = lf_ref.shape
    x = lf_ref[...].reshape(pages * heads, rows)
    parts = _split3(x)
    later = (_iota((rows, rows), 0) > _iota((rows, rows), 1)).astype(BF16)
    ones = jnp.ones((rows, rows), BF16)
    excl = tot = None
    for part in parts:
        e, t = _dot(part, later), _dot(part, ones)
        excl = e if excl is None else excl + e
        tot = t if tot is None else tot + t
    out_ref[:, 0:heads, :] = excl.reshape(pages, heads, rows)
    out_ref[:, heads:2 * heads, :] = tot.reshape(pages, heads, rows)


def _pool_suffix(logf_t):
    depth, n_pool, heads, rows = logf_t.shape
    return pl.pallas_call(
        _pool_suffix_kernel,
        grid=(depth, pl.cdiv(n_pool, POOL_BLOCK)),
        in_specs=[pl.BlockSpec((None, POOL_BLOCK, heads, rows), lambda l, i: (l, i, 0, 0))],
        out_specs=pl.BlockSpec((None, POOL_BLOCK, 2 * heads, rows), lambda l, i: (l, i, 0, 0)),
        out_shape=jax.ShapeDtypeStruct((depth, n_pool, 2 * heads, rows), F32),
        compiler_params=_cparams(("parallel", "parallel")),
        name="pool_suffix",
    )(logf_t)


def _fox_decode_kernel(pt_ref, q_ref, kn_ref, vn_ref, cn_ref, *rest):
    pps = PAGES_PER_STEP
    k_refs, v_refs, r_refs = rest[0:pps], rest[pps:2 * pps], rest[2 * pps:3 * pps]
    o_ref, m_ref, l_ref, acc_ref, suf_ref = rest[3 * pps:]
    j = pl.program_id(1)
    own = (_iota((A_HEADS, A_WIDTH), 1) // HEAD_DIM) == _iota((A_HEADS, A_WIDTH), 0)
    qbd = jnp.where(own, q_ref[...], 0.0)

    @pl.when(j == 0)
    def _():
        m_ref[...] = jnp.broadcast_to(jnp.sum(qbd * kn_ref[...], axis=1, keepdims=True), m_ref.shape)
        l_ref[...] = jnp.ones(l_ref.shape, F32)
        acc_ref[...] = jnp.broadcast_to(vn_ref[...], acc_ref.shape)
        suf_ref[...] = jnp.zeros(suf_ref.shape, F32)

    qb = qbd.astype(BF16)
    cn = cn_ref[...]
    m, l, acc, suf = m_ref[...], l_ref[...], acc_ref[...], suf_ref[...]

    def page_t(ref):
        return ref[...].reshape(A_WIDTH, ref.shape[-1]).astype(BF16)

    scores = []
    for u in range(pps):
        r = r_refs[u][...]
        s = (cn + suf + r[0:A_HEADS]) * LOG2E + _dot(qb, page_t(k_refs[u]))
        suf = suf + r[A_HEADS:2 * A_HEADS]
        scores.append(s)
    smax = scores[0]
    for s in scores[1:]:
        smax = jnp.maximum(smax, s)
    m_new = jnp.maximum(m, jnp.max(smax, axis=1, keepdims=True))
    a = jnp.exp2(m - m_new)
    probs = [jnp.exp2(s - m_new) for s in scores]
    psum = probs[0]
    for p in probs[1:]:
        psum = psum + p
    l = a * l + jnp.sum(psum, axis=1, keepdims=True)
    acc = acc * a[:, 0:1]
    for u in range(pps):
        acc = acc + _dot_nt(probs[u].astype(BF16), page_t(v_refs[u]))
    m_ref[...], l_ref[...], acc_ref[...], suf_ref[...] = m_new, l, acc, suf

    @pl.when(j == pl.num_programs(1) - 1)
    def _():
        o_ref[...] = jnp.sum(jnp.where(own, acc / l[:, 0:1], 0.0), axis=0, keepdims=True)


def _fox_decode(page_table, layer, q, k_new, v_new, logf_new, cache_kt, cache_vt, suffix):
    db, n_pages = page_table.shape
    pps = PAGES_PER_STEP
    assert n_pages % pps == 0
    steps = n_pages // pps
    rows = cache_kt.shape[-1]
    tok = pl.BlockSpec((None, 1, A_WIDTH), lambda b, j, pt: (b, 0, 0))

    def page(u, shape):
        zeros = (0,) * len(shape)
        return pl.BlockSpec((None, None) + shape,
                            lambda b, j, pt: (layer, pt[b, n_pages - 1 - (j * pps + u)]) + zeros)

    grid_spec = pltpu.PrefetchScalarGridSpec(
        num_scalar_prefetch=1,
        grid=(db, steps),
        in_specs=[tok, tok, tok, pl.BlockSpec((None, A_HEADS, LANES), lambda b, j, pt: (b, 0, 0))]
                 + [page(u, (A_HEADS, HEAD_DIM, rows)) for u in range(pps)]
                 + [page(u, (A_HEADS, HEAD_DIM, rows)) for u in range(pps)]
                 + [page(u, (2 * A_HEADS, rows)) for u in range(pps)],
        out_specs=tok,
        scratch_shapes=[pltpu.VMEM((A_HEADS, LANES), F32), pltpu.VMEM((A_HEADS, LANES), F32),
                        pltpu.VMEM((A_HEADS, A_WIDTH), F32), pltpu.VMEM((A_HEADS, LANES), F32)],
    )
    tok3 = lambda t: t.reshape(db, 1, A_WIDTH)
    return pl.pallas_call(
        _fox_decode_kernel,
        grid_spec=grid_spec,
        out_shape=jax.ShapeDtypeStruct((db, 1, A_WIDTH), F32),
        compiler_params=_cparams(("parallel", "arbitrary")),
        name="fox_decode",
    )(page_table, tok3(q), tok3(k_new), tok3(v_new), logf_new, *([cache_kt] * pps), *([cache_vt] * pps),
      *([suffix] * pps))


def _sample_mix_kernel(h_ref, cbuf_ref, dw_ref, dwb_ref, lng_ref, lnb_ref, pw_ref, pwb_ref,
                       x_ref, dbuf_ref, cw_ref, small_ref, alog_ref, dtb_ref,
                       ob_ref, q_ref, k_ref, v_ref, eg_ref, bt_ref, lf_ref):
    keep = CONF_WIDTH - 1
    h = h_ref[...]
    cbuf = cbuf_ref[...]
    acc = jnp.sum(cbuf * dw_ref[0:keep, :][None], axis=1) + dw_ref[keep:keep + 1, :] * h + dwb_ref[...]
    mu = jnp.mean(acc, axis=-1, keepdims=True)
    xc = acc - mu
    var = jnp.mean(xc * xc, axis=-1, keepdims=True)
    y = xc * lax.rsqrt(var + EPS) * lng_ref[...] + lnb_ref[...]
    ob_ref[...] = _dot(_silu(y).astype(BF16), pw_ref[...]) + pwb_ref[...]

    dk = DN_CONV - 1
    x = x_ref[...]
    dbuf = dbuf_ref[...]
    conv = jnp.sum(dbuf * cw_ref[0:dk, :][None], axis=1) + cw_ref[dk:dk + 1, :] * x
    yv = _silu(conv)
    qq, kk, vv = yv[:, :C_WIDTH], yv[:, C_WIDTH:2 * C_WIDTH], yv[:, 2 * C_WIDTH:]
    ones_bd = _head_sum_matrix()
    q_ref[...] = qq * lax.rsqrt(_head_sumsq(qq, ones_bd) + EPS) * (HEAD_DIM ** -0.5)
    k_ref[...] = kk * lax.rsqrt(_head_sumsq(kk, ones_bd) + EPS)
    v_ref[...] = vv
    sm = small_ref[...]
    eg_ref[...] = jnp.exp(-jnp.exp(alog_ref[...]) * _softplus(sm + dtb_ref[...]))
    bt_ref[...] = _sigmoid(sm)
    lf_ref[...] = _log_sigmoid(sm)


def _sample_mix(h, cbuf, dw, dwb, lng, lnb, pw, pwb, x, dbuf, cw, small, alog, dtb):
    db = h.shape[0]
    f = lambda *shape: jax.ShapeDtypeStruct(shape, F32)
    return pl.pallas_call(
        _sample_mix_kernel,
        out_shape=[f(db, B_WIDTH), f(db, C_WIDTH), f(db, C_WIDTH), f(db, C_WIDTH),
                   f(db, LANES), f(db, LANES), f(db, LANES)],
        compiler_params=pltpu.CompilerParams(vmem_limit_bytes=VMEM_LIMIT_BYTES),
        name="sample_mix",
    )(h, cbuf, dw, dwb, lng, lnb, pw, pwb, x, dbuf, cw, small, alog, dtb)


def _sample_delta_kernel(s_ref, qc_ref, kc_ref, qr_ref, kr_ref, v_ref, eg_ref, bt_ref, o_ref, snew_ref):
    sd = s_ref[...] * eg_ref[...]
    ks = jnp.sum(kc_ref[...] * sd, axis=1, keepdims=True)
    u = bt_ref[...] * (v_ref[...] - ks)
    qs = jnp.sum(qc_ref[...] * sd, axis=1, keepdims=True)
    qk = jnp.sum(qr_ref[...] * kr_ref[...], axis=2, keepdims=True)
    o_ref[...] = qs + qk * u
    snew_ref[...] = sd + kc_ref[...] * u


def _sample_delta(s, qn, kn, vn, eg, bt):
    db = s.shape[0]
    g = db * C_HEADS
    col = lambda t: t.reshape(g, HEAD_DIM, 1)
    row = lambda t: t.reshape(g, 1, HEAD_DIM)
    gate = lambda t, off: t[:, off:off + C_HEADS].reshape(g, 1, 1)
    o, s_new = pl.pallas_call(
        _sample_delta_kernel,
        out_shape=[jax.ShapeDtypeStruct((g, 1, HEAD_DIM), F32), jax.ShapeDtypeStruct((g, HEAD_DIM, HEAD_DIM), F32)],
        compiler_params=pltpu.CompilerParams(vmem_limit_bytes=VMEM_LIMIT_BYTES),
        name="sample_delta",
    )(s.reshape(g, HEAD_DIM, HEAD_DIM), col(qn), col(kn), row(qn), row(kn), row(vn), gate(eg, SM_A), gate(bt, SM_B))
    return o.reshape(db, C_WIDTH), s_new.reshape(db, C_HEADS, HEAD_DIM, HEAD_DIM)


def _permute_in_cols(t):
    aw, bw, cw = A_WIDTH, B_WIDTH, C_WIDTH
    f0 = 3 * aw
    glu0 = f0 + A_HEADS
    qkv0 = glu0 + 2 * bw
    a0 = qkv0 + 3 * cw
    b0 = a0 + C_HEADS
    z0 = b0 + C_HEADS
    parts = [t[..., 0:f0], t[..., glu0:qkv0], t[..., qkv0:a0], t[..., z0:z0 + cw],
             t[..., f0:glu0], t[..., a0:b0], t[..., b0:z0]]
    out = jnp.concatenate(parts, axis=-1)
    return jnp.pad(out, [(0, 0)] * (t.ndim - 1) + [(0, IN_COLS_PAD - out.shape[-1])])


def _lane_vec(vals, off):
    return jnp.zeros((1, LANES), F32).at[0, off:off + vals.shape[0]].set(vals)


def _row_tile(n):
    for tm in (384, 256, 128, 64, 32, 16, 8):
        if n % tm == 0:
            return tm
    raise ValueError(n)


def kernel(x_prompt, x_sample, cache_k, cache_v, cache_logf, state_conf_conv, state_dn_conv, state_dn_rec, page_table, meta_tokens, norm_mix, w_in, b_in, conf_dw, conf_dw_b, conf_ln_g, conf_ln_b, conf_pw, conf_pw_b, dn_conv, dn_a_log, dn_dt_bias, dn_norm_g, grp_norm_a, grp_norm_b, w_out, norm_ffn, w_ffn_gate, w_ffn_up, w_ffn_down, final_norm):
    batch, seq_in, d = x_prompt.shape
    db = x_sample.shape[0]
    assert x_sample.shape[1] == 1
    depth = w_in.shape[0]
    seq = NUM_META + seq_in
    n = batch * seq

    xp = jnp.concatenate([jnp.broadcast_to(meta_tokens[None], (batch, NUM_META, d)), x_prompt], axis=1).reshape(n, d)
    xs = x_sample.reshape(db, d)
    w_in_p = _permute_in_cols(w_in).astype(BF16)
    b_in_p = _permute_in_cols(b_in)
    tm_p, tm_s = _row_tile(n), _row_tile(db)
    fin = final_norm.reshape(1, d)
    r1 = lambda t: t.reshape(1, -1)
    suffix = _pool_suffix(jnp.transpose(cache_logf, (0, 1, 3, 2)))
    cache_kt = jnp.transpose(cache_k, (0, 1, 3, 4, 2))
    cache_vt = jnp.transpose(cache_v, (0, 1, 3, 4, 2))

    p_out = [[] for _ in range(6)]
    s_out = [[] for _ in range(6)]
    for l in range(depth):
        g_mix, g_ffn = r1(norm_mix[l]), r1(norm_ffn[l])
        wo, wg, wu, wd = (w_out[l].astype(BF16), w_ffn_gate[l].astype(BF16), w_ffn_up[l].astype(BF16),
                          w_ffn_down[l].astype(BF16))
        pw = conf_pw[l].astype(BF16)
        alog, dtb = _lane_vec(dn_a_log[l], SM_A), _lane_vec(dn_dt_bias[l], SM_A)
        ng_t = jnp.tile(dn_norm_g[l], C_HEADS).reshape(1, C_WIDTH)
        last = l == depth - 1
        conf_args = (conf_dw[l], r1(conf_dw_b[l]), r1(conf_ln_g[l]), r1(conf_ln_b[l]), pw, r1(conf_pw_b[l]))
        ffn_args = (r1(grp_norm_a[l]), r1(grp_norm_b[l]), ng_t, wo, g_ffn, wg, wu, wd, fin)

        qb, kb, vb, _, kt, vt, h, qkvc, z, small = _inproj(xp, g_mix, w_in_p[l], b_in_p[l:l + 1], tm_p, True)
        kf, vf = (t.reshape(A_HEADS, HEAD_DIM, batch, seq).transpose(2, 3, 0, 1) for t in (kt, vt))
        logf, qaug, kaug = _fox_gates(small, batch, seq)
        oa = _fox_prompt(qb, kb, vb, qaug, kaug, batch, seq)
        ob = _conformer(h, *conf_args, batch, seq)
        conf_buf = h.reshape(batch, seq, B_WIDTH)[:, seq - (CONF_WIDTH - 1):]
        dn_buf = qkvc.reshape(batch, seq, 3 * C_WIDTH)[:, seq - (DN_CONV - 1):]
        oc, s_rec = _deltanet(qkvc, small, dn_conv[l], alog, dtb, batch, seq)
        if last:
            y_prompt = _out_ffn_last(xp, oa, ob, oc, z, *ffn_args, batch, seq)
        else:
            xp = _out_ffn(xp, oa, ob, oc, z, *ffn_args, tm_p, False)
        for i, t in enumerate((kf, vf, logf, conf_buf, dn_buf, s_rec)):
            p_out[i].append(t)

        _, _, _, qf, kf, vf, h, qkvc, z, small = _inproj(xs, g_mix, w_in_p[l], b_in_p[l:l + 1], tm_s, False)
        conf_new = jnp.concatenate([state_conf_conv[l][:, 1:], h[:, None]], axis=1)
        dn_new = jnp.concatenate([state_dn_conv[l][:, 1:], qkvc[:, None]], axis=1)
        ob, qn, kn, vn, eg, bt, lf = _sample_mix(
            h, state_conf_conv[l], *conf_args, qkvc, state_dn_conv[l], dn_conv[l], small, alog, dtb)
        lf_heads = lf[:, SM_F:SM_F + A_HEADS]
        cn = jnp.broadcast_to(lf_heads[:, :, None], (db, A_HEADS, LANES))
        oa = _fox_decode(page_table, l, qf, kf, vf, cn, cache_kt, cache_vt, suffix)
        oa = jnp.transpose(oa.reshape(db, A_PAIRS, LANES), (1, 0, 2))
        oc, s_new = _sample_delta(state_dn_rec[l], qn, kn, vn, eg, bt)
        xs = _out_ffn(xs, oa, ob, oc, z, *ffn_args, tm_s, last)
        for i, t in enumerate((kf.reshape(db, 1, A_HEADS, HEAD_DIM), vf.reshape(db, 1, A_HEADS, HEAD_DIM),
                               lf_heads.reshape(db, 1, A_HEADS), conf_new, dn_new, s_new)):
            s_out[i].append(t)

    y_sample = xs.reshape(db, 1, d)
    return (y_prompt, y_sample, *[jnp.stack(a) for a in p_out], *[jnp.stack(a) for a in s_out])
```

```python
import functools

import jax
import jax.numpy as jnp
from jax import lax
from jax.experimental import pallas as pl
from jax.experimental.pallas import tpu as pltpu

F32 = jnp.float32
BF16 = jnp.bfloat16
HIGHEST = lax.Precision.HIGHEST

LANES = 128
SUBLANES = 8
VMEM_LIMIT_BYTES = 56 * 1024 * 1024

NUM_META = 16
A_HEADS = 8
HEAD_DIM = 64
A_WIDTH = A_HEADS * HEAD_DIM
A_PAIRS = A_WIDTH // LANES
B_WIDTH = 256
CONF_WIDTH = 31
C_HEADS = 4
C_WIDTH = C_HEADS * HEAD_DIM
DN_CONV = 4
DN_CHUNK = 64
EPS = 1e-6
NEG_BIG = -1e30
LOG2E = 1.4426950408889634

_Q0, _K0, _V0, _GLU0, _QKVC0, _Z0, _SM0 = 0, 512, 1024, 1536, 2048, 2816, 3072
IN_COLS_PAD = 3200
SM_F, SM_A, SM_B = 0, 8, 12


def _cparams(sem):
    return pltpu.CompilerParams(dimension_semantics=sem, vmem_limit_bytes=VMEM_LIMIT_BYTES)


def _const_spec(shape):
    nd = len(shape)
    return pl.BlockSpec(shape, lambda *_: (0,) * nd, pipeline_mode=pl.Buffered(1))


def _batched(a, b):
    if a.ndim == 2:
        a = jnp.broadcast_to(a, b.shape[:1] + a.shape)
    if b.ndim == 2:
        b = jnp.broadcast_to(b, a.shape[:1] + b.shape)
    return a, b


def _dot(a, b, precision=None):
    if a.ndim == 3 or b.ndim == 3:
        a, b = _batched(a, b)
        return lax.dot_general(a, b, (((2,), (1,)), ((0,), (0,))), preferred_element_type=F32, precision=precision)
    return jnp.dot(a, b, preferred_element_type=F32, precision=precision)


def _dot_nt(a, b, precision=None):
    if a.ndim == 3 or b.ndim == 3:
        a, b = _batched(a, b)
        return lax.dot_general(a, b, (((2,), (2,)), ((0,), (0,))), preferred_element_type=F32, precision=precision)
    return lax.dot_general(a, b, (((1,), (1,)), ((), ())), preferred_element_type=F32, precision=precision)


def _sigmoid(x):
    return 1.0 / (1.0 + jnp.exp(-x))


def _silu(x):
    return x * _sigmoid(x)


def _log_sigmoid(x):
    return jnp.minimum(x, 0.0) - jnp.log1p(jnp.exp(-jnp.abs(x)))


def _softplus(x):
    return jnp.maximum(x, 0.0) + jnp.log1p(jnp.exp(-jnp.abs(x)))


def _rms(x, g):
    return x * lax.rsqrt(jnp.mean(x * x, axis=-1, keepdims=True) + EPS) * g


def _iota(shape, dim):
    return lax.broadcasted_iota(jnp.int32, shape, dim)


def _split2(x):
    hi = x.astype(BF16)
    return hi, (x - hi.astype(F32)).astype(BF16)


def _split3(x):
    hi = x.astype(BF16)
    r = x - hi.astype(F32)
    mid = r.astype(BF16)
    return hi, mid, (r - mid.astype(F32)).astype(BF16)


def _dot_split(a, b):
    (ah, al), (bh, bl) = a, b
    return _dot(ah, bh) + (_dot(ah, bl) + _dot(al, bh))


def _lane_col(x, lane):
    return jnp.sum(jnp.where(_iota((1, x.shape[-1]), 1) == lane, x, 0.0), axis=-1, keepdims=True)


def _inproj_kernel(x_ref, g_ref, w_ref, b_ref, qb_ref, kb_ref, vb_ref, qf_ref, kf_ref, vf_ref,
                   h_ref, qkvc_ref, z_ref, small_ref):
    xn = _rms(x_ref[...], g_ref[...]).astype(BF16)

    def seg(lo, hi):
        return _dot(xn, w_ref[:, lo:hi]) + b_ref[:, lo:hi]

    q = seg(_Q0, _K0) * (HEAD_DIM ** -0.5 * LOG2E)
    k = seg(_K0, _V0)
    v = seg(_V0, _GLU0)
    qf_ref[...] = q
    kf_ref[...] = k
    vf_ref[...] = v
    for p in range(A_PAIRS):
        sl = slice(p * LANES, (p + 1) * LANES)
        qb_ref[p] = q[:, sl].astype(BF16)
        kb_ref[p] = k[:, sl].astype(BF16)
        vb_ref[p] = v[:, sl].astype(BF16)
    glu = seg(_GLU0, _QKVC0)
    h_ref[...] = glu[:, :B_WIDTH] * _sigmoid(glu[:, B_WIDTH:])
    qkvc_ref[...] = seg(_QKVC0, _Z0)
    z_ref[...] = seg(_Z0, _SM0)
    small_ref[...] = seg(_SM0, IN_COLS_PAD)


def _inproj(x, g, w, b, tm):
    n, d = x.shape
    row = lambda width: pl.BlockSpec((tm, width), lambda i: (i, 0))
    pair = pl.BlockSpec((A_PAIRS, tm, LANES), lambda i: (0, i, 0))
    pair_shape = jax.ShapeDtypeStruct((A_PAIRS, n, LANES), BF16)
    f = lambda width: jax.ShapeDtypeStruct((n, width), F32)
    kv, kv_shape = row(A_WIDTH), f(A_WIDTH)
    return pl.pallas_call(
        _inproj_kernel,
        grid=(n // tm,),
        in_specs=[row(d), _const_spec((1, d)), _const_spec(w.shape), _const_spec(b.shape)],
        out_specs=[pair, pair, pair, row(A_WIDTH), kv, kv, row(B_WIDTH),
                   row(3 * C_WIDTH), row(C_WIDTH), row(LANES)],
        out_shape=[pair_shape, pair_shape, pair_shape, f(A_WIDTH), kv_shape, kv_shape, f(B_WIDTH),
                   f(3 * C_WIDTH), f(C_WIDTH), f(LANES)],
        compiler_params=_cparams(("parallel",)),
        name="inproj",
    )(x, g, w, b)


AUG_PARTS = 3
AUG_HEAD = 2 * AUG_PARTS


def _fox_gates_kernel(small_ref, logf_ref, qaug_ref, kaug_ref, c_ref, *, seq):
    lf = _log_sigmoid(small_ref[...])
    logf_ref[...] = lf[:, SM_F:SM_F + A_HEADS]
    tril = (_iota((LANES, LANES), 0) >= _iota((LANES, LANES), 1)).astype(F32)
    carry = jnp.zeros((1, LANES), F32)
    full = seq // LANES
    for j in range(full):
        cs = _dot(tril, lf[j * LANES:(j + 1) * LANES], HIGHEST) + carry
        c_ref[j * LANES:(j + 1) * LANES, :] = cs
        carry = cs[LANES - 1:LANES]
    tail = seq - full * LANES
    if tail:
        c_ref[full * LANES:seq, :] = _dot(tril[:tail, :tail], lf[full * LANES:seq], HIGHEST) + carry

    parts = _split3(c_ref[...] * LOG2E)
    row, col = _iota((LANES, LANES), 0), _iota((LANES, LANES), 1)
    p3 = None
    for i, part in enumerate(parts):
        t = _dot(part, ((col == AUG_PARTS * row + i) & (row < A_HEADS)).astype(BF16))
        p3 = t if p3 is None else p3 + t
    p3 = p3.astype(BF16)
    lane = _iota((1, LANES), 1)
    slot = lane // AUG_PARTS
    for p in range(A_PAIRS):
        src = AUG_HEAD * p + (col // AUG_HEAD) * AUG_PARTS + col % AUG_PARTS
        hit = row == src
        cslot = col // AUG_PARTS
        gq = (hit & ((cslot == 0) | (cslot == 2))).astype(BF16)
        gk = (hit & ((cslot == 1) | (cslot == 3))).astype(BF16)
        one_q = ((slot == 1) | (slot == 3)).astype(F32)
        one_k = ((slot == 0) | (slot == 2)).astype(F32)
        qaug_ref[p] = (_dot(p3, gq) + one_q).astype(BF16)
        kaug_ref[p] = (one_k - _dot(p3, gk)).astype(BF16)


def _fox_gates(small, batch, seq):
    pair = pl.BlockSpec((A_PAIRS, seq, LANES), lambda b: (0, b, 0))
    pair_shape = jax.ShapeDtypeStruct((A_PAIRS, batch * seq, LANES), BF16)
    return pl.pallas_call(
        functools.partial(_fox_gates_kernel, seq=seq),
        grid=(batch,),
        in_specs=[pl.BlockSpec((seq, LANES), lambda b: (b, 0))],
        out_specs=[pl.BlockSpec((None, seq, A_HEADS), lambda b: (b, 0, 0)), pair, pair],
        out_shape=[jax.ShapeDtypeStruct((batch, seq, A_HEADS), F32), pair_shape, pair_shape],
        scratch_shapes=[pltpu.VMEM((seq, LANES), F32)],
        compiler_params=_cparams(("parallel",)),
        name="fox_gates",
    )(small)


Q_ROWS = 512


def _fox_prompt_kernel(q_ref, k_ref, v_ref, qa_ref, ka_ref, o_ref, kcat_ref, v0_ref, v1_ref, *, seq):
    lane = _iota((1, LANES), 1)
    lo = lane < HEAD_DIM
    zero = jnp.zeros((), BF16)
    kcat_ref[:, 0:LANES] = k_ref[...]
    kcat_ref[:, LANES:2 * LANES] = ka_ref[...]
    v = v_ref[...]
    v0_ref[...] = jnp.where(lo, v, zero)
    v1_ref[...] = jnp.where(lo, zero, v)
    heads = ((lo, lane < AUG_HEAD, v0_ref), (~lo, (lane >= AUG_HEAD) & (lane < 2 * AUG_HEAD), v1_ref))

    def block(r0, nq):
        q, qa = q_ref[r0:r0 + nq, :], qa_ref[r0:r0 + nq, :]
        causal = _iota((nq, nq), 0) >= _iota((nq, nq), 1)
        out = None
        for lanes_h, aug_h, vh_ref in heads:
            qc = jnp.concatenate([jnp.where(lanes_h, q, zero), jnp.where(aug_h, qa, zero)], axis=1)
            s_d = jnp.where(causal, _dot_nt(qc, kcat_ref[r0:r0 + nq, :]), NEG_BIG)
            m = jnp.max(s_d, axis=1, keepdims=True)
            if r0:
                s_f = _dot_nt(qc, kcat_ref[0:r0, :])
                m = jnp.maximum(m, jnp.max(s_f, axis=1, keepdims=True))
            p_d = jnp.exp2(s_d - m)
            l = jnp.sum(p_d, axis=1, keepdims=True)
            o = _dot(p_d.astype(BF16), vh_ref[r0:r0 + nq, :])
            if r0:
                p_f = jnp.exp2(s_f - m)
                l = l + jnp.sum(p_f, axis=1, keepdims=True)
                o = o + _dot(p_f.astype(BF16), vh_ref[0:r0, :])
            o = o * (1.0 / l)
            out = o if out is None else out + o
        o_ref[r0:r0 + nq, :] = out

    full = seq // Q_ROWS
    for i in range(full):
        block(i * Q_ROWS, Q_ROWS)
    if seq - full * Q_ROWS:
        block(full * Q_ROWS, seq - full * Q_ROWS)


def _fox_prompt(qb, kb, vb, qaug, kaug, batch, seq):
    n = batch * seq
    tile = pl.BlockSpec((None, seq, LANES), lambda b, p: (p, b, 0))
    return pl.pallas_call(
        functools.partial(_fox_prompt_kernel, seq=seq),
        grid=(batch, A_PAIRS),
        in_specs=[tile] * 5,
        out_specs=tile,
        out_shape=jax.ShapeDtypeStruct((A_PAIRS, n, LANES), F32),
        scratch_shapes=[pltpu.VMEM((seq, 2 * LANES), BF16), pltpu.VMEM((seq, LANES), BF16),
                        pltpu.VMEM((seq, LANES), BF16)],
        compiler_params=_cparams(("parallel", "parallel")),
        name="fox_prompt",
    )(qb, kb, vb, qaug, kaug)


CONF_ROWS = 48
CONF_PAD = 32


def _conformer_kernel(h_ref, dw_ref, dwb_ref, lng_ref, lnb_ref, pw_ref, pwb_ref, ob_ref, hp_ref, *, seq):
    hp_ref[0:CONF_PAD, :] = jnp.zeros((CONF_PAD, B_WIDTH), F32)
    hp_ref[CONF_PAD:CONF_PAD + seq, :] = h_ref[...]
    shift = CONF_PAD - (CONF_WIDTH - 1)

    def chunk(c):
        r0 = c * CONF_ROWS
        if not isinstance(r0, int):
            r0 = pl.multiple_of(r0, SUBLANES)
        win = hp_ref[pl.ds(r0, CONF_ROWS + CONF_PAD), :]
        rows = CONF_ROWS + CONF_PAD
        phases = [win] + [pltpu.roll(win, rows - r, axis=0) for r in range(1, SUBLANES)]
        acc = jnp.zeros((CONF_ROWS, B_WIDTH), F32) + dwb_ref[...]
        for w in range(CONF_WIDTH):
            base, r = (shift + w) // SUBLANES * SUBLANES, (shift + w) % SUBLANES
            acc = acc + dw_ref[w:w + 1, :] * phases[r][base:base + CONF_ROWS, :]
        mu = jnp.mean(acc, axis=-1, keepdims=True)
        xc = acc - mu
        var = jnp.mean(xc * xc, axis=-1, keepdims=True)
        y = xc * lax.rsqrt(var + EPS) * lng_ref[...] + lnb_ref[...]
        ob_ref[pl.ds(r0, CONF_ROWS), :] = _dot(_silu(y).astype(BF16), pw_ref[...]) + pwb_ref[...]

    def pair(i, carry):
        chunk(2 * i)
        chunk(2 * i + 1)
        return carry

    n_chunks = seq // CONF_ROWS
    lax.fori_loop(0, n_chunks // 2, pair, 0)
    if n_chunks % 2:
        chunk(n_chunks - 1)


def _conformer(h, dw, dwb, lng, lnb, pw, pwb, batch, seq):
    assert seq % CONF_ROWS == 0
    vec = _const_spec((1, B_WIDTH))
    return pl.pallas_call(
        functools.partial(_conformer_kernel, seq=seq),
        grid=(batch,),
        in_specs=[pl.BlockSpec((seq, B_WIDTH), lambda b: (b, 0)), _const_spec(dw.shape), vec, vec, vec,
                  _const_spec(pw.shape), vec],
        out_specs=pl.BlockSpec((seq, B_WIDTH), lambda b: (b, 0)),
        out_shape=jax.ShapeDtypeStruct((batch * seq, B_WIDTH), F32),
        scratch_shapes=[pltpu.VMEM((CONF_PAD + seq, B_WIDTH), F32)],
        compiler_params=_cparams(("parallel",)),
        name="conformer",
    )(h, dw, dwb, lng, lnb, pw, pwb)


DN_PAD = 8
DN_PREP_ROWS = 344


def _divisor_rows(seq, cap):
    return max(r for r in range(SUBLANES, cap + 1, SUBLANES) if seq % r == 0)


def _head_sum_matrix():
    return ((_iota((C_WIDTH, C_WIDTH), 0) // HEAD_DIM) == (_iota((C_WIDTH, C_WIDTH), 1) // HEAD_DIM)).astype(BF16)


def _head_sumsq(x, ones_bd):
    hi, lo = _split2(x * x)
    return _dot(hi, ones_bd) + _dot(lo, ones_bd)


def _unit_lower_inverse(ns, size):
    n = ns.shape[-1]
    eye = (_iota((n, n), 0) == _iota((n, n), 1)).astype(F32)
    inv = eye - ns
    pw = ns.astype(BF16)
    k = 2
    while k < size:
        pw = _dot(pw, pw).astype(BF16)
        inv = inv + _dot(inv.astype(BF16), pw)
        k *= 2
    resid = (eye - inv) - _dot_split(_split2(ns), _split2(inv))
    return _split2(inv + _dot(inv.astype(BF16), resid.astype(BF16)))


def _expand_heads(x):
    head = _iota((1, x.shape[-1]), 1) // HEAD_DIM
    return jnp.concatenate([jnp.where(head == h, x, 0.0) for h in range(x.shape[-1] // HEAD_DIM)], axis=-2)


def _delta_chunk(q, k, v, gb, s_bd, chunk):
    g = bt = gb
    c = chunk
    width = q.shape[-1]
    heads = width // HEAD_DIM
    n = heads * c
    lead = q.shape[:-2]
    tril_c = (_iota((c, c), 0) >= _iota((c, c), 1)).astype(F32)
    gcum = _dot(tril_c, g, HIGHEST)
    last = gcum[..., c - 1:c, :]
    rows_cat = lambda parts: jnp.concatenate(parts, axis=-2)
    gcol = rows_cat([_lane_col(gcum, SM_A + h) for h in range(heads)])
    bcol = rows_cat([_lane_col(bt, SM_B + h) for h in range(heads)])
    glast = rows_cat([jnp.broadcast_to(_lane_col(last, SM_A + h), lead + (c, 1)) for h in range(heads)])
    gmask = rows_cat([jnp.where(_iota((1, LANES), 1) == SM_A + h, gcum, 0.0) for h in range(heads)])
    ones_b = jnp.ones((n, LANES), BF16)
    grow = None
    for part in _split3(gmask):
        t = _dot_nt(ones_b, part)
        grow = t if grow is None else grow + t
    ri, ci = _iota((n, n), 0), _iota((n, n), 1)
    same = (ri // c) == (ci // c)
    tri = same & (ri >= ci)
    strict = same & (ri > ci)
    decay = jnp.where(tri, jnp.exp(jnp.where(tri, gcol - grow, 0.0)), 0.0)

    kx, qx, vx = _expand_heads(k), _expand_heads(q), _expand_heads(v)
    kb = kx * bcol
    kxb = kx.astype(BF16)
    a_strict = jnp.where(strict, _dot_nt(kb.astype(BF16), kxb) * decay, 0.0)
    inv = _unit_lower_inverse(a_strict, c)
    eg = jnp.exp(gcol)
    sol = _dot_split(inv, _split2(jnp.concatenate([vx * bcol, kb * eg], axis=-1)))
    u0, w = sol[..., :width], sol[..., width:]
    qk = _dot_nt(qx.astype(BF16), kxb) * decay
    q_dec = qx * eg
    k_dec = kx * jnp.exp(glast - gcol)
    sb = s_bd.astype(BF16)
    u = u0 - _dot(w.astype(BF16), sb)
    ub = u.astype(BF16)
    o = _dot(q_dec.astype(BF16), sb) + _dot(qk.astype(BF16), ub)
    gl_rows = rows_cat([jnp.broadcast_to(jnp.exp(_lane_col(last, SM_A + h)), lead + (HEAD_DIM, 1))
                        for h in range(heads)])
    s_new = s_bd * gl_rows + _dot(jnp.swapaxes(k_dec, -1, -2).astype(BF16), ub)
    o_c = o[..., 0:c, :]
    for h in range(1, heads):
        o_c = o_c + o[..., h * c:(h + 1) * c, :]
    return o_c, s_new


def _dn_prep_kernel(x_ref, small_ref, cw_ref, alog_ref, dtb_ref, q_ref, k_ref, v_ref, gb_ref, xp_ref, *, seq):
    keep = DN_CONV - 1
    xp_ref[0:DN_PAD, :] = jnp.zeros((DN_PAD, 3 * C_WIDTH), F32)
    xp_ref[DN_PAD:DN_PAD + seq, :] = x_ref[...]
    ones_bd = _head_sum_matrix()
    sm = small_ref[...]
    lane = _iota((1, LANES), 1)
    g = -jnp.exp(alog_ref[...]) * _softplus(sm + dtb_ref[...])
    gb_ref[...] = jnp.where((lane >= SM_A) & (lane < SM_A + C_HEADS), g, _sigmoid(sm))

    shift = DN_PAD - keep
    prep = _divisor_rows(seq, DN_PREP_ROWS)
    for c in range(seq // prep):
        r0 = c * prep
        acc = None
        for w in range(DN_CONV):
            t = cw_ref[w:w + 1, :] * xp_ref[r0 + shift + w:r0 + shift + w + prep, :]
            acc = t if acc is None else acc + t
        y = _silu(acc)
        qq, kk, vv = y[:, :C_WIDTH], y[:, C_WIDTH:2 * C_WIDTH], y[:, 2 * C_WIDTH:]
        rows = slice(r0, r0 + prep)
        q_ref[rows, :] = qq * lax.rsqrt(_head_sumsq(qq, ones_bd) + EPS) * (HEAD_DIM ** -0.5)
        k_ref[rows, :] = kk * lax.rsqrt(_head_sumsq(kk, ones_bd) + EPS)
        v_ref[rows, :] = vv


def _dn_scan_kernel(q_ref, k_ref, v_ref, gb_ref, oc_ref, srec_ref, *, seq, group):
    pairs = C_WIDTH // LANES
    pair_heads = LANES // HEAD_DIM
    problems = [(b, p) for b in range(group) for p in range(pairs)]

    def run_chunk(r0, chunk, states):
        rows = [pl.ds(b * seq + r0, chunk) for b in range(group)]
        lanes = [slice(p * LANES, (p + 1) * LANES) for p in range(pairs)]
        stack = lambda ref: jnp.stack([ref[rows[b], lanes[p]] for b, p in problems])

        def gates(b, p):
            gb = gb_ref[rows[b], :]
            return gb if p == 0 else pltpu.roll(gb, LANES - p * pair_heads, axis=1)

        o, states = _delta_chunk(stack(q_ref), stack(k_ref), stack(v_ref),
                                 jnp.stack([gates(b, p) for b, p in problems]), states, chunk)
        for i, (b, p) in enumerate(problems):
            oc_ref[rows[b], lanes[p]] = o[i]
        return states

    states = run_chunk(0, NUM_META, jnp.zeros((len(problems), LANES, LANES), F32))

    def body(i, states):
        return run_chunk(pl.multiple_of(NUM_META + i * DN_CHUNK, SUBLANES), DN_CHUNK, states)

    states = lax.fori_loop(0, (seq - NUM_META) // DN_CHUNK, body, states)
    for i, (b, p) in enumerate(problems):
        s_bd = states[i]
        fold = s_bd + pltpu.roll(s_bd, HEAD_DIM, axis=1)
        srec_ref[b, p * pair_heads:(p + 1) * pair_heads] = fold[:, 0:HEAD_DIM].reshape(pair_heads, HEAD_DIM, HEAD_DIM)


DN_GROUP = 4


def _deltanet(qkvc, small, cw, alog, dtb, batch, seq):
    n = batch * seq
    vec = _const_spec((1, LANES))
    rows = lambda width: pl.BlockSpec((seq, width), lambda b: (b, 0))
    f = lambda width: jax.ShapeDtypeStruct((n, width), F32)
    q, k, v, gb = pl.pallas_call(
        functools.partial(_dn_prep_kernel, seq=seq),
        grid=(batch,),
        in_specs=[rows(3 * C_WIDTH), rows(LANES), _const_spec(cw.shape), vec, vec],
        out_specs=[rows(C_WIDTH), rows(C_WIDTH), rows(C_WIDTH), rows(LANES)],
        out_shape=[f(C_WIDTH), f(C_WIDTH), f(C_WIDTH), f(LANES)],
        scratch_shapes=[pltpu.VMEM((DN_PAD + seq, 3 * C_WIDTH), F32)],
        compiler_params=_cparams(("parallel",)),
        name="dn_prep",
    )(qkvc, small, cw, alog, dtb)

    group = max(g for g in range(1, DN_GROUP + 1) if batch % g == 0)
    once = lambda width: pl.BlockSpec((group * seq, width), lambda b: (b, 0), pipeline_mode=pl.Buffered(1))
    return pl.pallas_call(
        functools.partial(_dn_scan_kernel, seq=seq, group=group),
        grid=(batch // group,),
        in_specs=[once(C_WIDTH), once(C_WIDTH), once(C_WIDTH), once(LANES)],
        out_specs=[once(C_WIDTH),
                   pl.BlockSpec((group, C_HEADS, HEAD_DIM, HEAD_DIM), lambda b: (b, 0, 0, 0))],
        out_shape=[f(C_WIDTH), jax.ShapeDtypeStruct((batch, C_HEADS, HEAD_DIM, HEAD_DIM), F32)],
        compiler_params=_cparams(("parallel",)),
        name="dn_scan",
    )(q, k, v, gb)


def _out_ffn_kernel(x_ref, oa_ref, ob_ref, oc_ref, z_ref, ga_ref, gb_ref, gc_ref, wo_ref, gf_ref, wg_ref, wu_ref,
                    wd_ref, fin_ref, y_ref, *, final):
    oa = jnp.concatenate([oa_ref[p] for p in range(A_PAIRS)], axis=-1)
    oc = oc_ref[...]
    ms = _head_sumsq(oc, _head_sum_matrix()) * (1.0 / HEAD_DIM)
    oc = oc * lax.rsqrt(ms + EPS) * gc_ref[...] * _silu(z_ref[...])
    mixed = jnp.concatenate([_rms(oa, ga_ref[...]), _rms(ob_ref[...], gb_ref[...]), oc], axis=-1)
    x1 = x_ref[...] + _dot(mixed.astype(BF16), wo_ref[...])
    xn = _rms(x1, gf_ref[...]).astype(BF16)
    hid = _silu(_dot(xn, wg_ref[...])) * _dot(xn, wu_ref[...])
    x2 = x1 + _dot(hid.astype(BF16), wd_ref[...])
    y_ref[...] = _rms(x2, fin_ref[...]) if final else x2


def _out_ffn(x, oa, ob, oc, z, ga, gb, gc, wo, gf, wg, wu, wd, fin, tm, final):
    n, d = x.shape
    row = lambda width: pl.BlockSpec((tm, width), lambda i: (i, 0))
    consts = (ga, gb, gc, wo, gf, wg, wu, wd, fin)
    return pl.pallas_call(
        functools.partial(_out_ffn_kernel, final=final),
        grid=(n // tm,),
        in_specs=[row(d), pl.BlockSpec((A_PAIRS, tm, LANES), lambda i: (0, i, 0)), row(B_WIDTH),
                  row(C_WIDTH), row(C_WIDTH)] + [_const_spec(c.shape) for c in consts],
        out_specs=row(d),
        out_shape=jax.ShapeDtypeStruct((n, d), F32),
        compiler_params=_cparams(("parallel",)),
        name="out_ffn",
    )(x, oa, ob, oc, z, *consts)


def _out_ffn_last(x, oa, ob, oc, z, ga, gb, gc, wo, gf, wg, wu, wd, fin, batch, seq):
    d = x.shape[1]
    out_rows = seq - NUM_META
    tm = max(t for t in (512, 256, 128, 64, 32, 16) if out_rows % t == 0)
    start = lambda b, j: pl.multiple_of(b * seq + NUM_META + j * tm, NUM_META)
    row = lambda width: pl.BlockSpec((pl.Element(tm), pl.Element(width)), lambda b, j: (start(b, j), 0))
    pair = pl.BlockSpec((pl.Element(A_PAIRS), pl.Element(tm), pl.Element(LANES)), lambda b, j: (0, start(b, j), 0))
    consts = (ga, gb, gc, wo, gf, wg, wu, wd, fin)
    return pl.pallas_call(
        functools.partial(_out_ffn_kernel, final=True),
        grid=(batch, out_rows // tm),
        in_specs=[row(d), pair, row(B_WIDTH), row(C_WIDTH), row(C_WIDTH)] + [_const_spec(c.shape) for c in consts],
        out_specs=pl.BlockSpec((None, tm, d), lambda b, j: (b, j, 0)),
        out_shape=jax.ShapeDtypeStruct((batch, out_rows, d), F32),
        compiler_params=_cparams(("parallel", "parallel")),
        name="out_ffn_last",
    )(x, oa, ob, oc, z, *consts)


POOL_BLOCK = 256
PAGES_PER_STEP = 32


def _pool_suffix_kernel(lf_ref, out_ref):
    pages, heads, rows = lf_ref.shape
    x = lf_ref[...].reshape(pages * heads, rows)
    parts = _split3(x)
    later = (_iota((rows, rows), 0) > _iota((rows, rows), 1)).astype(BF16)
    ones = jnp.ones((rows, rows), BF16)
    excl = tot = None
    for part in parts:
        e, t = _dot(part, later), _dot(part, ones)
        excl = e if excl is None else excl + e
        tot = t if tot is None else tot + t
    out_ref[:, 0:heads, :] = excl.reshape(pages, heads, rows)
    out_ref[:, heads:2 * heads, :] = tot.reshape(pages, heads, rows)


def _pool_suffix(logf_t):
    depth, n_pool, heads, rows = logf_t.shape
    return pl.pallas_call(
        _pool_suffix_kernel,
        grid=(depth, pl.cdiv(n_pool, POOL_BLOCK)),
        in_specs=[pl.BlockSpec((None, POOL_BLOCK, heads, rows), lambda l, i: (l, i, 0, 0))],
        out_specs=pl.BlockSpec((None, POOL_BLOCK, 2 * heads, rows), lambda l, i: (l, i, 0, 0)),
        out_shape=jax.ShapeDtypeStruct((depth, n_pool, 2 * heads, rows), F32),
        compiler_params=_cparams(("parallel", "parallel")),
        name="pool_suffix",
    )(logf_t)


def _fox_decode_kernel(pt_ref, q_ref, kn_ref, vn_ref, cn_ref, *rest):
    pps = PAGES_PER_STEP
    k_refs, v_refs, r_refs = rest[0:pps], rest[pps:2 * pps], rest[2 * pps:3 * pps]
    o_ref, m_ref, l_ref, acc_ref, suf_ref = rest[3 * pps:]
    j = pl.program_id(1)
    own = (_iota((A_HEADS, A_WIDTH), 1) // HEAD_DIM) == _iota((A_HEADS, A_WIDTH), 0)
    qbd = jnp.where(own, q_ref[...], 0.0)

    @pl.when(j == 0)
    def _():
        m_ref[...] = jnp.broadcast_to(jnp.sum(qbd * kn_ref[...], axis=1, keepdims=True), m_ref.shape)
        l_ref[...] = jnp.ones(l_ref.shape, F32)
        acc_ref[...] = jnp.broadcast_to(vn_ref[...], acc_ref.shape)
        suf_ref[...] = jnp.zeros(suf_ref.shape, F32)

    qb = qbd.astype(BF16)
    cn = cn_ref[...]
    m, l, acc, suf = m_ref[...], l_ref[...], acc_ref[...], suf_ref[...]

    def page_t(ref):
        return ref[...].reshape(A_WIDTH, ref.shape[-1]).astype(BF16)

    scores = []
    for u in range(pps):
        r = r_refs[u][...]
        s = (cn + suf + r[0:A_HEADS]) * LOG2E + _dot(qb, page_t(k_refs[u]))
        suf = suf + r[A_HEADS:2 * A_HEADS]
        scores.append(s)
    smax = scores[0]
    for s in scores[1:]:
        smax = jnp.maximum(smax, s)
    m_new = jnp.maximum(m, jnp.max(smax, axis=1, keepdims=True))
    a = jnp.exp2(m - m_new)
    probs = [jnp.exp2(s - m_new) for s in scores]
    psum = probs[0]
    for p in probs[1:]:
        psum = psum + p
    l = a * l + jnp.sum(psum, axis=1, keepdims=True)
    acc = acc * a[:, 0:1]
    for u in range(pps):
        acc = acc + _dot_nt(probs[u].astype(BF16), page_t(v_refs[u]))
    m_ref[...], l_ref[...], acc_ref[...], suf_ref[...] = m_new, l, acc, suf

    @pl.when(j == pl.num_programs(1) - 1)
    def _():
        o_ref[...] = jnp.sum(jnp.where(own, acc / l[:, 0:1], 0.0), axis=0, keepdims=True)


def _fox_decode(page_table, layer, q, k_new, v_new, logf_new, cache_kt, cache_vt, suffix):
    db, n_pages = page_table.shape
    pps = PAGES_PER_STEP
    assert n_pages % pps == 0
    steps = n_pages // pps
    rows = cache_kt.shape[-1]
    tok = pl.BlockSpec((None, 1, A_WIDTH), lambda b, j, pt: (b, 0, 0))

    def page(u, shape):
        zeros = (0,) * len(shape)
        return pl.BlockSpec((None, None) + shape,
                            lambda b, j, pt: (layer, pt[b, n_pages - 1 - (j * pps + u)]) + zeros)

    grid_spec = pltpu.PrefetchScalarGridSpec(
        num_scalar_prefetch=1,
        grid=(db, steps),
        in_specs=[tok, tok, tok, pl.BlockSpec((None, A_HEADS, LANES), lambda b, j, pt: (b, 0, 0))]
                 + [page(u, (A_HEADS, HEAD_DIM, rows)) for u in range(pps)]
                 + [page(u, (A_HEADS, HEAD_DIM, rows)) for u in range(pps)]
                 + [page(u, (2 * A_HEADS, rows)) for u in range(pps)],
        out_specs=tok,
        scratch_shapes=[pltpu.VMEM((A_HEADS, LANES), F32), pltpu.VMEM((A_HEADS, LANES), F32),
                        pltpu.VMEM((A_HEADS, A_WIDTH), F32), pltpu.VMEM((A_HEADS, LANES), F32)],
    )
    tok3 = lambda t: t.reshape(db, 1, A_WIDTH)
    return pl.pallas_call(
        _fox_decode_kernel,
        grid_spec=grid_spec,
        out_shape=jax.ShapeDtypeStruct((db, 1, A_WIDTH), F32),
        compiler_params=_cparams(("parallel", "arbitrary")),
        name="fox_decode",
    )(page_table, tok3(q), tok3(k_new), tok3(v_new), logf_new, *([cache_kt] * pps), *([cache_vt] * pps),
      *([suffix] * pps))


def _sample_mix_kernel(h_ref, cbuf_ref, dw_ref, dwb_ref, lng_ref, lnb_ref, pw_ref, pwb_ref,
                       x_ref, dbuf_ref, cw_ref, small_ref, alog_ref, dtb_ref,
                       ob_ref, q_ref, k_ref, v_ref, eg_ref, bt_ref, lf_ref):
    keep = CONF_WIDTH - 1
    h = h_ref[...]
    cbuf = cbuf_ref[...]
    acc = jnp.sum(cbuf * dw_ref[0:keep, :][None], axis=1) + dw_ref[keep:keep + 1, :] * h + dwb_ref[...]
    mu = jnp.mean(acc, axis=-1, keepdims=True)
    xc = acc - mu
    var = jnp.mean(xc * xc, axis=-1, keepdims=True)
    y = xc * lax.rsqrt(var + EPS) * lng_ref[...] + lnb_ref[...]
    ob_ref[...] = _dot(_silu(y).astype(BF16), pw_ref[...]) + pwb_ref[...]

    dk = DN_CONV - 1
    x = x_ref[...]
    dbuf = dbuf_ref[...]
    conv = jnp.sum(dbuf * cw_ref[0:dk, :][None], axis=1) + cw_ref[dk:dk + 1, :] * x
    yv = _silu(conv)
    qq, kk, vv = yv[:, :C_WIDTH], yv[:, C_WIDTH:2 * C_WIDTH], yv[:, 2 * C_WIDTH:]
    ones_bd = _head_sum_matrix()
    q_ref[...] = qq * lax.rsqrt(_head_sumsq(qq, ones_bd) + EPS) * (HEAD_DIM ** -0.5)
    k_ref[...] = kk * lax.rsqrt(_head_sumsq(kk, ones_bd) + EPS)
    v_ref[...] = vv
    sm = small_ref[...]
    eg_ref[...] = jnp.exp(-jnp.exp(alog_ref[...]) * _softplus(sm + dtb_ref[...]))
    bt_ref[...] = _sigmoid(sm)
    lf_ref[...] = _log_sigmoid(sm)


def _sample_mix(h, cbuf, dw, dwb, lng, lnb, pw, pwb, x, dbuf, cw, small, alog, dtb):
    db = h.shape[0]
    f = lambda *shape: jax.ShapeDtypeStruct(shape, F32)
    return pl.pallas_call(
        _sample_mix_kernel,
        out_shape=[f(db, B_WIDTH), f(db, C_WIDTH), f(db, C_WIDTH), f(db, C_WIDTH),
                   f(db, LANES), f(db, LANES), f(db, LANES)],
        compiler_params=pltpu.CompilerParams(vmem_limit_bytes=VMEM_LIMIT_BYTES),
        name="sample_mix",
    )(h, cbuf, dw, dwb, lng, lnb, pw, pwb, x, dbuf, cw, small, alog, dtb)


def _sample_delta_kernel(s_ref, qc_ref, kc_ref, qr_ref, kr_ref, v_ref, eg_ref, bt_ref, o_ref, snew_ref):
    sd = s_ref[...] * eg_ref[...]
    ks = jnp.sum(kc_ref[...] * sd, axis=1, keepdims=True)
    u = bt_ref[...] * (v_ref[...] - ks)
    qs = jnp.sum(qc_ref[...] * sd, axis=1, keepdims=True)
    qk = jnp.sum(qr_ref[...] * kr_ref[...], axis=2, keepdims=True)
    o_ref[...] = qs + qk * u
    snew_ref[...] = sd + kc_ref[...] * u


def _sample_delta(s, qn, kn, vn, eg, bt):
    db = s.shape[0]
    g = db * C_HEADS
    col = lambda t: t.reshape(g, HEAD_DIM, 1)
    row = lambda t: t.reshape(g, 1, HEAD_DIM)
    gate = lambda t, off: t[:, off:off + C_HEADS].reshape(g, 1, 1)
    o, s_new = pl.pallas_call(
        _sample_delta_kernel,
        out_shape=[jax.ShapeDtypeStruct((g, 1, HEAD_DIM), F32), jax.ShapeDtypeStruct((g, HEAD_DIM, HEAD_DIM), F32)],
        compiler_params=pltpu.CompilerParams(vmem_limit_bytes=VMEM_LIMIT_BYTES),
        name="sample_delta",
    )(s.reshape(g, HEAD_DIM, HEAD_DIM), col(qn), col(kn), row(qn), row(kn), row(vn), gate(eg, SM_A), gate(bt, SM_B))
    return o.reshape(db, C_WIDTH), s_new.reshape(db, C_HEADS, HEAD_DIM, HEAD_DIM)


def _permute_in_cols(t):
    aw, bw, cw = A_WIDTH, B_WIDTH, C_WIDTH
    f0 = 3 * aw
    glu0 = f0 + A_HEADS
    qkv0 = glu0 + 2 * bw
    a0 = qkv0 + 3 * cw
    b0 = a0 + C_HEADS
    z0 = b0 + C_HEADS
    parts = [t[..., 0:f0], t[..., glu0:qkv0], t[..., qkv0:a0], t[..., z0:z0 + cw],
             t[..., f0:glu0], t[..., a0:b0], t[..., b0:z0]]
    out = jnp.concatenate(parts, axis=-1)
    return jnp.pad(out, [(0, 0)] * (t.ndim - 1) + [(0, IN_COLS_PAD - out.shape[-1])])


def _lane_vec(vals, off):
    return jnp.zeros((1, LANES), F32).at[0, off:off + vals.shape[0]].set(vals)


def _row_tile(n):
    for tm in (384, 256, 128, 64, 32, 16, 8):
        if n % tm == 0:
            return tm
    raise ValueError(n)


def kernel(x_prompt, x_sample, cache_k, cache_v, cache_logf, state_conf_conv, state_dn_conv, state_dn_rec, page_table, meta_tokens, norm_mix, w_in, b_in, conf_dw, conf_dw_b, conf_ln_g, conf_ln_b, conf_pw, conf_pw_b, dn_conv, dn_a_log, dn_dt_bias, dn_norm_g, grp_norm_a, grp_norm_b, w_out, norm_ffn, w_ffn_gate, w_ffn_up, w_ffn_down, final_norm):
    batch, seq_in, d = x_prompt.shape
    db = x_sample.shape[0]
    assert x_sample.shape[1] == 1
    depth = w_in.shape[0]
    seq = NUM_META + seq_in
    n = batch * seq

    xp = jnp.concatenate([jnp.broadcast_to(meta_tokens[None], (batch, NUM_META, d)), x_prompt], axis=1).reshape(n, d)
    xs = x_sample.reshape(db, d)
    w_in_p = _permute_in_cols(w_in).astype(BF16)
    b_in_p = _permute_in_cols(b_in)
    tm_p, tm_s = _row_tile(n), _row_tile(db)
    fin = final_norm.reshape(1, d)
    r1 = lambda t: t.reshape(1, -1)
    suffix = _pool_suffix(jnp.transpose(cache_logf, (0, 1, 3, 2)))
    cache_kt = jnp.transpose(cache_k, (0, 1, 3, 4, 2))
    cache_vt = jnp.transpose(cache_v, (0, 1, 3, 4, 2))

    p_out = [[] for _ in range(6)]
    s_out = [[] for _ in range(6)]
    for l in range(depth):
        g_mix, g_ffn = r1(norm_mix[l]), r1(norm_ffn[l])
        wo, wg, wu, wd = (w_out[l].astype(BF16), w_ffn_gate[l].astype(BF16), w_ffn_up[l].astype(BF16),
                          w_ffn_down[l].astype(BF16))
        pw = conf_pw[l].astype(BF16)
        alog, dtb = _lane_vec(dn_a_log[l], SM_A), _lane_vec(dn_dt_bias[l], SM_A)
        ng_t = jnp.tile(dn_norm_g[l], C_HEADS).reshape(1, C_WIDTH)
        last = l == depth - 1
        conf_args = (conf_dw[l], r1(conf_dw_b[l]), r1(conf_ln_g[l]), r1(conf_ln_b[l]), pw, r1(conf_pw_b[l]))
        ffn_args = (r1(grp_norm_a[l]), r1(grp_norm_b[l]), ng_t, wo, g_ffn, wg, wu, wd, fin)

        qb, kb, vb, _, kf, vf, h, qkvc, z, small = _inproj(xp, g_mix, w_in_p[l], b_in_p[l:l + 1], tm_p)
        kf, vf = (t.reshape(batch, seq, A_HEADS, HEAD_DIM) for t in (kf, vf))
        logf, qaug, kaug = _fox_gates(small, batch, seq)
        oa = _fox_prompt(qb, kb, vb, qaug, kaug, batch, seq)
        ob = _conformer(h, *conf_args, batch, seq)
        conf_buf = h.reshape(batch, seq, B_WIDTH)[:, seq - (CONF_WIDTH - 1):]
        dn_buf = qkvc.reshape(batch, seq, 3 * C_WIDTH)[:, seq - (DN_CONV - 1):]
        oc, s_rec = _deltanet(qkvc, small, dn_conv[l], alog, dtb, batch, seq)
        if last:
            y_prompt = _out_ffn_last(xp, oa, ob, oc, z, *ffn_args, batch, seq)
        else:
            xp = _out_ffn(xp, oa, ob, oc, z, *ffn_args, tm_p, False)
        for i, t in enumerate((kf, vf, logf, conf_buf, dn_buf, s_rec)):
            p_out[i].append(t)

        _, _, _, qf, kf, vf, h, qkvc, z, small = _inproj(xs, g_mix, w_in_p[l], b_in_p[l:l + 1], tm_s)
        conf_new = jnp.concatenate([state_conf_conv[l][:, 1:], h[:, None]], axis=1)
        dn_new = jnp.concatenate([state_dn_conv[l][:, 1:], qkvc[:, None]], axis=1)
        ob, qn, kn, vn, eg, bt, lf = _sample_mix(
            h, state_conf_conv[l], *conf_args, qkvc, state_dn_conv[l], dn_conv[l], small, alog, dtb)
        lf_heads = lf[:, SM_F:SM_F + A_HEADS]
        cn = jnp.broadcast_to(lf_heads[:, :, None], (db, A_HEADS, LANES))
        oa = _fox_decode(page_table, l, qf, kf, vf, cn, cache_kt, cache_vt, suffix)
        oa = jnp.transpose(oa.reshape(db, A_PAIRS, LANES), (1, 0, 2))
        oc, s_new = _sample_delta(state_dn_rec[l], qn, kn, vn, eg, bt)
        xs = _out_ffn(xs, oa, ob, oc, z, *ffn_args, tm_s, last)
        for i, t in enumerate((kf.reshape(db, 1, A_HEADS, HEAD_DIM), vf.reshape(db, 1, A_HEADS, HEAD_DIM),
                               lf_heads.reshape(db, 1, A_HEADS), conf_new, dn_new, s_new)):
            s_out[i].append(t)

    y_sample = xs.reshape(db, 1, d)
    return (y_prompt, y_sample, *[jnp.stack(a) for a in p_out], *[jnp.stack(a) for a in s_out])
```

```python
import functools

import jax
import jax.numpy as jnp
from jax import lax
from jax.experimental import pallas as pl
from jax.experimental.pallas import tpu as pltpu

F32 = jnp.float32
BF16 = jnp.bfloat16
HIGHEST = lax.Precision.HIGHEST

LANES = 128
SUBLANES = 8
VMEM_LIMIT_BYTES = 56 * 1024 * 1024

NUM_META = 16
A_HEADS = 8
HEAD_DIM = 64
A_WIDTH = A_HEADS * HEAD_DIM
A_PAIRS = A_WIDTH // LANES
B_WIDTH = 256
CONF_WIDTH = 31
C_HEADS = 4
C_WIDTH = C_HEADS * HEAD_DIM
DN_CONV = 4
DN_CHUNK = 64
EPS = 1e-6
NEG_BIG = -1e30
LOG2E = 1.4426950408889634

_Q0, _K0, _V0, _GLU0, _QKVC0, _Z0, _SM0 = 0, 512, 1024, 1536, 2048, 2816, 3072
IN_COLS_PAD = 3200
SM_F, SM_A, SM_B = 0, 8, 12


def _cparams(sem):
    return pltpu.CompilerParams(dimension_semantics=sem, vmem_limit_bytes=VMEM_LIMIT_BYTES)


def _const_spec(shape):
    nd = len(shape)
    return pl.BlockSpec(shape, lambda *_: (0,) * nd, pipeline_mode=pl.Buffered(1))


def _batched(a, b):
    if a.ndim == 2:
        a = jnp.broadcast_to(a, b.shape[:1] + a.shape)
    if b.ndim == 2:
        b = jnp.broadcast_to(b, a.shape[:1] + b.shape)
    return a, b


def _dot(a, b, precision=None):
    if a.ndim == 3 or b.ndim == 3:
        a, b = _batched(a, b)
        return lax.dot_general(a, b, (((2,), (1,)), ((0,), (0,))), preferred_element_type=F32, precision=precision)
    return jnp.dot(a, b, preferred_element_type=F32, precision=precision)


def _dot_nt(a, b, precision=None):
    if a.ndim == 3 or b.ndim == 3:
        a, b = _batched(a, b)
        return lax.dot_general(a, b, (((2,), (2,)), ((0,), (0,))), preferred_element_type=F32, precision=precision)
    return lax.dot_general(a, b, (((1,), (1,)), ((), ())), preferred_element_type=F32, precision=precision)


def _sigmoid(x):
    return 1.0 / (1.0 + jnp.exp(-x))


def _silu(x):
    return x * _sigmoid(x)


def _log_sigmoid(x):
    return jnp.minimum(x, 0.0) - jnp.log1p(jnp.exp(-jnp.abs(x)))


def _softplus(x):
    return jnp.maximum(x, 0.0) + jnp.log1p(jnp.exp(-jnp.abs(x)))


def _rms(x, g):
    return x * lax.rsqrt(jnp.mean(x * x, axis=-1, keepdims=True) + EPS) * g


def _iota(shape, dim):
    return lax.broadcasted_iota(jnp.int32, shape, dim)


def _split2(x):
    hi = x.astype(BF16)
    return hi, (x - hi.astype(F32)).astype(BF16)


def _split3(x):
    hi = x.astype(BF16)
    r = x - hi.astype(F32)
    mid = r.astype(BF16)
    return hi, mid, (r - mid.astype(F32)).astype(BF16)


def _dot_split(a, b):
    (ah, al), (bh, bl) = a, b
    return _dot(ah, bh) + (_dot(ah, bl) + _dot(al, bh))


def _lane_col(x, lane):
    return jnp.sum(jnp.where(_iota((1, x.shape[-1]), 1) == lane, x, 0.0), axis=-1, keepdims=True)


def _inproj_kernel(x_ref, g_ref, w_ref, b_ref, qb_ref, kb_ref, vb_ref, qf_ref, kf_ref, vf_ref,
                   h_ref, qkvc_ref, z_ref, small_ref):
    xn = _rms(x_ref[...], g_ref[...]).astype(BF16)

    def seg(lo, hi):
        return _dot(xn, w_ref[:, lo:hi]) + b_ref[:, lo:hi]

    q = seg(_Q0, _K0) * (HEAD_DIM ** -0.5 * LOG2E)
    k = seg(_K0, _V0)
    v = seg(_V0, _GLU0)
    qf_ref[...] = q
    kf_ref[...] = k
    vf_ref[...] = v
    for p in range(A_PAIRS):
        sl = slice(p * LANES, (p + 1) * LANES)
        qb_ref[p] = q[:, sl].astype(BF16)
        kb_ref[p] = k[:, sl].astype(BF16)
        vb_ref[p] = v[:, sl].astype(BF16)
    glu = seg(_GLU0, _QKVC0)
    h_ref[...] = glu[:, :B_WIDTH] * _sigmoid(glu[:, B_WIDTH:])
    qkvc_ref[...] = seg(_QKVC0, _Z0)
    z_ref[...] = seg(_Z0, _SM0)
    small_ref[...] = seg(_SM0, IN_COLS_PAD)


def _inproj(x, g, w, b, tm):
    n, d = x.shape
    row = lambda width: pl.BlockSpec((tm, width), lambda i: (i, 0))
    pair = pl.BlockSpec((A_PAIRS, tm, LANES), lambda i: (0, i, 0))
    pair_shape = jax.ShapeDtypeStruct((A_PAIRS, n, LANES), BF16)
    f = lambda width: jax.ShapeDtypeStruct((n, width), F32)
    kv, kv_shape = row(A_WIDTH), f(A_WIDTH)
    return pl.pallas_call(
        _inproj_kernel,
        grid=(n // tm,),
        in_specs=[row(d), _const_spec((1, d)), _const_spec(w.shape), _const_spec(b.shape)],
        out_specs=[pair, pair, pair, row(A_WIDTH), kv, kv, row(B_WIDTH),
                   row(3 * C_WIDTH), row(C_WIDTH), row(LANES)],
        out_shape=[pair_shape, pair_shape, pair_shape, f(A_WIDTH), kv_shape, kv_shape, f(B_WIDTH),
                   f(3 * C_WIDTH), f(C_WIDTH), f(LANES)],
        compiler_params=_cparams(("parallel",)),
        name="inproj",
    )(x, g, w, b)


AUG_PARTS = 3
AUG_HEAD = 2 * AUG_PARTS


def _fox_gates_kernel(small_ref, logf_ref, qaug_ref, kaug_ref, c_ref, *, seq):
    lf = _log_sigmoid(small_ref[...])
    logf_ref[...] = lf[:, SM_F:SM_F + A_HEADS]
    tril = (_iota((LANES, LANES), 0) >= _iota((LANES, LANES), 1)).astype(F32)
    carry = jnp.zeros((1, LANES), F32)
    full = seq // LANES
    for j in range(full):
        cs = _dot(tril, lf[j * LANES:(j + 1) * LANES], HIGHEST) + carry
        c_ref[j * LANES:(j + 1) * LANES, :] = cs
        carry = cs[LANES - 1:LANES]
    tail = seq - full * LANES
    if tail:
        c_ref[full * LANES:seq, :] = _dot(tril[:tail, :tail], lf[full * LANES:seq], HIGHEST) + carry

    parts = _split3(c_ref[...] * LOG2E)
    row, col = _iota((LANES, LANES), 0), _iota((LANES, LANES), 1)
    p3 = None
    for i, part in enumerate(parts):
        t = _dot(part, ((col == AUG_PARTS * row + i) & (row < A_HEADS)).astype(BF16))
        p3 = t if p3 is None else p3 + t
    p3 = p3.astype(BF16)
    lane = _iota((1, LANES), 1)
    slot = lane // AUG_PARTS
    for p in range(A_PAIRS):
        src = AUG_HEAD * p + (col // AUG_HEAD) * AUG_PARTS + col % AUG_PARTS
        hit = row == src
        cslot = col // AUG_PARTS
        gq = (hit & ((cslot == 0) | (cslot == 2))).astype(BF16)
        gk = (hit & ((cslot == 1) | (cslot == 3))).astype(BF16)
        one_q = ((slot == 1) | (slot == 3)).astype(F32)
        one_k = ((slot == 0) | (slot == 2)).astype(F32)
        qaug_ref[p] = (_dot(p3, gq) + one_q).astype(BF16)
        kaug_ref[p] = (one_k - _dot(p3, gk)).astype(BF16)


def _fox_gates(small, batch, seq):
    pair = pl.BlockSpec((A_PAIRS, seq, LANES), lambda b: (0, b, 0))
    pair_shape = jax.ShapeDtypeStruct((A_PAIRS, batch * seq, LANES), BF16)
    return pl.pallas_call(
        functools.partial(_fox_gates_kernel, seq=seq),
        grid=(batch,),
        in_specs=[pl.BlockSpec((seq, LANES), lambda b: (b, 0))],
        out_specs=[pl.BlockSpec((None, seq, A_HEADS), lambda b: (b, 0, 0)), pair, pair],
        out_shape=[jax.ShapeDtypeStruct((batch, seq, A_HEADS), F32), pair_shape, pair_shape],
        scratch_shapes=[pltpu.VMEM((seq, LANES), F32)],
        compiler_params=_cparams(("parallel",)),
        name="fox_gates",
    )(small)


Q_ROWS = 512


def _fox_prompt_kernel(q_ref, k_ref, v_ref, qa_ref, ka_ref, o_ref, kcat_ref, v0_ref, v1_ref, *, seq):
    lane = _iota((1, LANES), 1)
    lo = lane < HEAD_DIM
    zero = jnp.zeros((), BF16)
    kcat_ref[:, 0:LANES] = k_ref[...]
    kcat_ref[:, LANES:2 * LANES] = ka_ref[...]
    v = v_ref[...]
    v0_ref[...] = jnp.where(lo, v, zero)
    v1_ref[...] = jnp.where(lo, zero, v)
    heads = ((lo, lane < AUG_HEAD, v0_ref), (~lo, (lane >= AUG_HEAD) & (lane < 2 * AUG_HEAD), v1_ref))

    def block(r0, nq):
        q, qa = q_ref[r0:r0 + nq, :], qa_ref[r0:r0 + nq, :]
        causal = _iota((nq, nq), 0) >= _iota((nq, nq), 1)
        out = None
        for lanes_h, aug_h, vh_ref in heads:
            qc = jnp.concatenate([jnp.where(lanes_h, q, zero), jnp.where(aug_h, qa, zero)], axis=1)
            s_d = jnp.where(causal, _dot_nt(qc, kcat_ref[r0:r0 + nq, :]), NEG_BIG)
            m = jnp.max(s_d, axis=1, keepdims=True)
            if r0:
                s_f = _dot_nt(qc, kcat_ref[0:r0, :])
                m = jnp.maximum(m, jnp.max(s_f, axis=1, keepdims=True))
            p_d = jnp.exp2(s_d - m)
            l = jnp.sum(p_d, axis=1, keepdims=True)
            o = _dot(p_d.astype(BF16), vh_ref[r0:r0 + nq, :])
            if r0:
                p_f = jnp.exp2(s_f - m)
                l = l + jnp.sum(p_f, axis=1, keepdims=True)
                o = o + _dot(p_f.astype(BF16), vh_ref[0:r0, :])
            o = o * (1.0 / l)
            out = o if out is None else out + o
        o_ref[r0:r0 + nq, :] = out

    full = seq // Q_ROWS
    for i in range(full):
        block(i * Q_ROWS, Q_ROWS)
    if seq - full * Q_ROWS:
        block(full * Q_ROWS, seq - full * Q_ROWS)


def _fox_prompt(qb, kb, vb, qaug, kaug, batch, seq):
    n = batch * seq
    tile = pl.BlockSpec((None, seq, LANES), lambda b, p: (p, b, 0))
    return pl.pallas_call(
        functools.partial(_fox_prompt_kernel, seq=seq),
        grid=(batch, A_PAIRS),
        in_specs=[tile] * 5,
        out_specs=tile,
        out_shape=jax.ShapeDtypeStruct((A_PAIRS, n, LANES), F32),
        scratch_shapes=[pltpu.VMEM((seq, 2 * LANES), BF16), pltpu.VMEM((seq, LANES), BF16),
                        pltpu.VMEM((seq, LANES), BF16)],
        compiler_params=_cparams(("parallel", "parallel")),
        name="fox_prompt",
    )(qb, kb, vb, qaug, kaug)


CONF_ROWS = 48
CONF_PAD = 32
CONF_GROUP = 4


def _conformer_kernel(h_ref, dw_ref, dwb_ref, lng_ref, lnb_ref, pw_ref, pwb_ref, ob_ref, hp_ref, *, seq):
    hp_ref[0:CONF_PAD, :] = jnp.zeros((CONF_PAD, B_WIDTH), F32)
    hp_ref[CONF_PAD:CONF_PAD + seq, :] = h_ref[...]
    shift = CONF_PAD - (CONF_WIDTH - 1)

    def chunk(c):
        r0 = c * CONF_ROWS
        if not isinstance(r0, int):
            r0 = pl.multiple_of(r0, SUBLANES)
        win = hp_ref[pl.ds(r0, CONF_ROWS + CONF_PAD), :]
        rows = CONF_ROWS + CONF_PAD
        phases = [win] + [pltpu.roll(win, rows - r, axis=0) for r in range(1, SUBLANES)]
        acc = jnp.zeros((CONF_ROWS, B_WIDTH), F32) + dwb_ref[...]
        for w in range(CONF_WIDTH):
            base, r = (shift + w) // SUBLANES * SUBLANES, (shift + w) % SUBLANES
            acc = acc + dw_ref[w:w + 1, :] * phases[r][base:base + CONF_ROWS, :]
        mu = jnp.mean(acc, axis=-1, keepdims=True)
        xc = acc - mu
        var = jnp.mean(xc * xc, axis=-1, keepdims=True)
        y = xc * lax.rsqrt(var + EPS) * lng_ref[...] + lnb_ref[...]
        ob_ref[pl.ds(r0, CONF_ROWS), :] = _dot(_silu(y).astype(BF16), pw_ref[...]) + pwb_ref[...]

    def group(i, carry):
        for j in range(CONF_GROUP):
            chunk(CONF_GROUP * i + j)
        return carry

    n_chunks = seq // CONF_ROWS
    lax.fori_loop(0, n_chunks // CONF_GROUP, group, 0)
    for c in range(n_chunks - n_chunks % CONF_GROUP, n_chunks):
        chunk(c)


def _conformer(h, dw, dwb, lng, lnb, pw, pwb, batch, seq):
    assert seq % CONF_ROWS == 0
    vec = _const_spec((1, B_WIDTH))
    return pl.pallas_call(
        functools.partial(_conformer_kernel, seq=seq),
        grid=(batch,),
        in_specs=[pl.BlockSpec((seq, B_WIDTH), lambda b: (b, 0)), _const_spec(dw.shape), vec, vec, vec,
                  _const_spec(pw.shape), vec],
        out_specs=pl.BlockSpec((seq, B_WIDTH), lambda b: (b, 0)),
        out_shape=jax.ShapeDtypeStruct((batch * seq, B_WIDTH), F32),
        scratch_shapes=[pltpu.VMEM((CONF_PAD + seq, B_WIDTH), F32)],
        compiler_params=_cparams(("parallel",)),
        name="conformer",
    )(h, dw, dwb, lng, lnb, pw, pwb)


DN_PAD = 8
DN_PREP_ROWS = 344


def _divisor_rows(seq, cap):
    return max(r for r in range(SUBLANES, cap + 1, SUBLANES) if seq % r == 0)


def _head_sum_matrix():
    return ((_iota((C_WIDTH, C_WIDTH), 0) // HEAD_DIM) == (_iota((C_WIDTH, C_WIDTH), 1) // HEAD_DIM)).astype(BF16)


def _head_sumsq(x, ones_bd):
    hi, lo = _split2(x * x)
    return _dot(hi, ones_bd) + _dot(lo, ones_bd)


def _unit_lower_inverse(ns, size):
    n = ns.shape[-1]
    eye = (_iota((n, n), 0) == _iota((n, n), 1)).astype(F32)
    inv = eye - ns
    pw = ns.astype(BF16)
    k = 2
    while k < size:
        pw = _dot(pw, pw).astype(BF16)
        inv = inv + _dot(inv.astype(BF16), pw)
        k *= 2
    resid = (eye - inv) - _dot_split(_split2(ns), _split2(inv))
    return _split2(inv + _dot(inv.astype(BF16), resid.astype(BF16)))


def _expand_heads(x):
    head = _iota((1, x.shape[-1]), 1) // HEAD_DIM
    return jnp.concatenate([jnp.where(head == h, x, 0.0) for h in range(x.shape[-1] // HEAD_DIM)], axis=-2)


def _delta_chunk(q, k, v, gb, s_bd, chunk):
    g = bt = gb
    c = chunk
    width = q.shape[-1]
    heads = width // HEAD_DIM
    n = heads * c
    lead = q.shape[:-2]
    tril_c = (_iota((c, c), 0) >= _iota((c, c), 1)).astype(F32)
    gcum = _dot(tril_c, g, HIGHEST)
    last = gcum[..., c - 1:c, :]
    rows_cat = lambda parts: jnp.concatenate(parts, axis=-2)
    gcol = rows_cat([_lane_col(gcum, SM_A + h) for h in range(heads)])
    bcol = rows_cat([_lane_col(bt, SM_B + h) for h in range(heads)])
    glast = rows_cat([jnp.broadcast_to(_lane_col(last, SM_A + h), lead + (c, 1)) for h in range(heads)])
    gmask = rows_cat([jnp.where(_iota((1, LANES), 1) == SM_A + h, gcum, 0.0) for h in range(heads)])
    ones_b = jnp.ones((n, LANES), BF16)
    grow = None
    for part in _split3(gmask):
        t = _dot_nt(ones_b, part)
        grow = t if grow is None else grow + t
    ri, ci = _iota((n, n), 0), _iota((n, n), 1)
    same = (ri // c) == (ci // c)
    tri = same & (ri >= ci)
    strict = same & (ri > ci)
    decay = jnp.where(tri, jnp.exp(jnp.where(tri, gcol - grow, 0.0)), 0.0)

    kx, qx, vx = _expand_heads(k), _expand_heads(q), _expand_heads(v)
    kb = kx * bcol
    kxb = kx.astype(BF16)
    a_strict = jnp.where(strict, _dot_nt(kb.astype(BF16), kxb) * decay, 0.0)
    inv = _unit_lower_inverse(a_strict, c)
    eg = jnp.exp(gcol)
    sol = _dot_split(inv, _split2(jnp.concatenate([vx * bcol, kb * eg], axis=-1)))
    u0, w = sol[..., :width], sol[..., width:]
    qk = _dot_nt(qx.astype(BF16), kxb) * decay
    q_dec = qx * eg
    k_dec = kx * jnp.exp(glast - gcol)
    sb = s_bd.astype(BF16)
    u = u0 - _dot(w.astype(BF16), sb)
    ub = u.astype(BF16)
    o = _dot(q_dec.astype(BF16), sb) + _dot(qk.astype(BF16), ub)
    gl_rows = rows_cat([jnp.broadcast_to(jnp.exp(_lane_col(last, SM_A + h)), lead + (HEAD_DIM, 1))
                        for h in range(heads)])
    s_new = s_bd * gl_rows + _dot(jnp.swapaxes(k_dec, -1, -2).astype(BF16), ub)
    o_c = o[..., 0:c, :]
    for h in range(1, heads):
        o_c = o_c + o[..., h * c:(h + 1) * c, :]
    return o_c, s_new


def _dn_prep_kernel(x_ref, small_ref, cw_ref, alog_ref, dtb_ref, q_ref, k_ref, v_ref, gb_ref, xp_ref, *, seq):
    keep = DN_CONV - 1
    xp_ref[0:DN_PAD, :] = jnp.zeros((DN_PAD, 3 * C_WIDTH), F32)
    xp_ref[DN_PAD:DN_PAD + seq, :] = x_ref[...]
    ones_bd = _head_sum_matrix()
    sm = small_ref[...]
    lane = _iota((1, LANES), 1)
    g = -jnp.exp(alog_ref[...]) * _softplus(sm + dtb_ref[...])
    gb_ref[...] = jnp.where((lane >= SM_A) & (lane < SM_A + C_HEADS), g, _sigmoid(sm))

    shift = DN_PAD - keep
    prep = _divisor_rows(seq, DN_PREP_ROWS)
    for c in range(seq // prep):
        r0 = c * prep
        acc = None
        for w in range(DN_CONV):
            t = cw_ref[w:w + 1, :] * xp_ref[r0 + shift + w:r0 + shift + w + prep, :]
            acc = t if acc is None else acc + t
        y = _silu(acc)
        qq, kk, vv = y[:, :C_WIDTH], y[:, C_WIDTH:2 * C_WIDTH], y[:, 2 * C_WIDTH:]
        rows = slice(r0, r0 + prep)
        q_ref[rows, :] = qq * lax.rsqrt(_head_sumsq(qq, ones_bd) + EPS) * (HEAD_DIM ** -0.5)
        k_ref[rows, :] = kk * lax.rsqrt(_head_sumsq(kk, ones_bd) + EPS)
        v_ref[rows, :] = vv


def _dn_scan_kernel(q_ref, k_ref, v_ref, gb_ref, oc_ref, srec_ref, *, seq, group):
    pairs = C_WIDTH // LANES
    pair_heads = LANES // HEAD_DIM
    problems = [(b, p) for b in range(group) for p in range(pairs)]

    def run_chunk(r0, chunk, states):
        rows = [pl.ds(b * seq + r0, chunk) for b in range(group)]
        lanes = [slice(p * LANES, (p + 1) * LANES) for p in range(pairs)]
        stack = lambda ref: jnp.stack([ref[rows[b], lanes[p]] for b, p in problems])

        def gates(b, p):
            gb = gb_ref[rows[b], :]
            return gb if p == 0 else pltpu.roll(gb, LANES - p * pair_heads, axis=1)

        o, states = _delta_chunk(stack(q_ref), stack(k_ref), stack(v_ref),
                                 jnp.stack([gates(b, p) for b, p in problems]), states, chunk)
        for i, (b, p) in enumerate(problems):
            oc_ref[rows[b], lanes[p]] = o[i]
        return states

    states = run_chunk(0, NUM_META, jnp.zeros((len(problems), LANES, LANES), F32))

    def body(i, states):
        return run_chunk(pl.multiple_of(NUM_META + i * DN_CHUNK, SUBLANES), DN_CHUNK, states)

    states = lax.fori_loop(0, (seq - NUM_META) // DN_CHUNK, body, states)
    for i, (b, p) in enumerate(problems):
        s_bd = states[i]
        fold = s_bd + pltpu.roll(s_bd, HEAD_DIM, axis=1)
        srec_ref[b, p * pair_heads:(p + 1) * pair_heads] = fold[:, 0:HEAD_DIM].reshape(pair_heads, HEAD_DIM, HEAD_DIM)


DN_GROUP = 4


def _deltanet(qkvc, small, cw, alog, dtb, batch, seq):
    n = batch * seq
    vec = _const_spec((1, LANES))
    rows = lambda width: pl.BlockSpec((seq, width), lambda b: (b, 0))
    f = lambda width: jax.ShapeDtypeStruct((n, width), F32)
    q, k, v, gb = pl.pallas_call(
        functools.partial(_dn_prep_kernel, seq=seq),
        grid=(batch,),
        in_specs=[rows(3 * C_WIDTH), rows(LANES), _const_spec(cw.shape), vec, vec],
        out_specs=[rows(C_WIDTH), rows(C_WIDTH), rows(C_WIDTH), rows(LANES)],
        out_shape=[f(C_WIDTH), f(C_WIDTH), f(C_WIDTH), f(LANES)],
        scratch_shapes=[pltpu.VMEM((DN_PAD + seq, 3 * C_WIDTH), F32)],
        compiler_params=_cparams(("parallel",)),
        name="dn_prep",
    )(qkvc, small, cw, alog, dtb)

    group = max(g for g in range(1, DN_GROUP + 1) if batch % g == 0)
    once = lambda width: pl.BlockSpec((group * seq, width), lambda b: (b, 0), pipeline_mode=pl.Buffered(1))
    return pl.pallas_call(
        functools.partial(_dn_scan_kernel, seq=seq, group=group),
        grid=(batch // group,),
        in_specs=[once(C_WIDTH), once(C_WIDTH), once(C_WIDTH), once(LANES)],
        out_specs=[once(C_WIDTH),
                   pl.BlockSpec((group, C_HEADS, HEAD_DIM, HEAD_DIM), lambda b: (b, 0, 0, 0))],
        out_shape=[f(C_WIDTH), jax.ShapeDtypeStruct((batch, C_HEADS, HEAD_DIM, HEAD_DIM), F32)],
        compiler_params=_cparams(("parallel",)),
        name="dn_scan",
    )(q, k, v, gb)


def _out_ffn_kernel(x_ref, oa_ref, ob_ref, oc_ref, z_ref, ga_ref, gb_ref, gc_ref, wo_ref, gf_ref, wg_ref, wu_ref,
                    wd_ref, fin_ref, y_ref, *, final):
    oa = jnp.concatenate([oa_ref[p] for p in range(A_PAIRS)], axis=-1)
    oc = oc_ref[...]
    ms = _head_sumsq(oc, _head_sum_matrix()) * (1.0 / HEAD_DIM)
    oc = oc * lax.rsqrt(ms + EPS) * gc_ref[...] * _silu(z_ref[...])
    mixed = jnp.concatenate([_rms(oa, ga_ref[...]), _rms(ob_ref[...], gb_ref[...]), oc], axis=-1)
    x1 = x_ref[...] + _dot(mixed.astype(BF16), wo_ref[...])
    xn = _rms(x1, gf_ref[...]).astype(BF16)
    hid = _silu(_dot(xn, wg_ref[...])) * _dot(xn, wu_ref[...])
    x2 = x1 + _dot(hid.astype(BF16), wd_ref[...])
    y_ref[...] = _rms(x2, fin_ref[...]) if final else x2


def _out_ffn(x, oa, ob, oc, z, ga, gb, gc, wo, gf, wg, wu, wd, fin, tm, final):
    n, d = x.shape
    row = lambda width: pl.BlockSpec((tm, width), lambda i: (i, 0))
    consts = (ga, gb, gc, wo, gf, wg, wu, wd, fin)
    return pl.pallas_call(
        functools.partial(_out_ffn_kernel, final=final),
        grid=(n // tm,),
        in_specs=[row(d), pl.BlockSpec((A_PAIRS, tm, LANES), lambda i: (0, i, 0)), row(B_WIDTH),
                  row(C_WIDTH), row(C_WIDTH)] + [_const_spec(c.shape) for c in consts],
        out_specs=row(d),
        out_shape=jax.ShapeDtypeStruct((n, d), F32),
        compiler_params=_cparams(("parallel",)),
        name="out_ffn",
    )(x, oa, ob, oc, z, *consts)


def _out_ffn_last(x, oa, ob, oc, z, ga, gb, gc, wo, gf, wg, wu, wd, fin, batch, seq):
    d = x.shape[1]
    out_rows = seq - NUM_META
    tm = max(t for t in (512, 256, 128, 64, 32, 16) if out_rows % t == 0)
    start = lambda b, j: pl.multiple_of(b * seq + NUM_META + j * tm, NUM_META)
    row = lambda width: pl.BlockSpec((pl.Element(tm), pl.Element(width)), lambda b, j: (start(b, j), 0))
    pair = pl.BlockSpec((pl.Element(A_PAIRS), pl.Element(tm), pl.Element(LANES)), lambda b, j: (0, start(b, j), 0))
    consts = (ga, gb, gc, wo, gf, wg, wu, wd, fin)
    return pl.pallas_call(
        functools.partial(_out_ffn_kernel, final=True),
        grid=(batch, out_rows // tm),
        in_specs=[row(d), pair, row(B_WIDTH), row(C_WIDTH), row(C_WIDTH)] + [_const_spec(c.shape) for c in consts],
        out_specs=pl.BlockSpec((None, tm, d), lambda b, j: (b, j, 0)),
        out_shape=jax.ShapeDtypeStruct((batch, out_rows, d), F32),
        compiler_params=_cparams(("parallel", "parallel")),
        name="out_ffn_last",
    )(x, oa, ob, oc, z, *consts)


POOL_BLOCK = 256
PAGES_PER_STEP = 32


def _pool_suffix_kernel(lf_ref, out_ref):
    pages, heads, rows = lf_ref.shape
    x = lf_ref[...].reshape(pages * heads, rows)
    parts = _split3(x)
    later = (_iota((rows, rows), 0) > _iota((rows, rows), 1)).astype(BF16)
    ones = jnp.ones((rows, rows), BF16)
    excl = tot = None
    for part in parts:
        e, t = _dot(part, later), _dot(part, ones)
        excl = e if excl is None else excl + e
        tot = t if tot is None else tot + t
    out_ref[:, 0:heads, :] = excl.reshape(pages, heads, rows)
    out_ref[:, heads:2 * heads, :] = tot.reshape(pages, heads, rows)


def _pool_suffix(logf_t):
    depth, n_pool, heads, rows = logf_t.shape
    return pl.pallas_call(
        _pool_suffix_kernel,
        grid=(depth, pl.cdiv(n_pool, POOL_BLOCK)),
        in_specs=[pl.BlockSpec((None, POOL_BLOCK, heads, rows), lambda l, i: (l, i, 0, 0))],
        out_specs=pl.BlockSpec((None, POOL_BLOCK, 2 * heads, rows), lambda l, i: (l, i, 0, 0)),
        out_shape=jax.ShapeDtypeStruct((depth, n_pool, 2 * heads, rows), F32),
        compiler_params=_cparams(("parallel", "parallel")),
        name="pool_suffix",
    )(logf_t)


def _fox_decode_kernel(pt_ref, q_ref, kn_ref, vn_ref, cn_ref, *rest):
    pps = PAGES_PER_STEP
    k_refs, v_refs, r_refs = rest[0:pps], rest[pps:2 * pps], rest[2 * pps:3 * pps]
    o_ref, m_ref, l_ref, acc_ref, suf_ref = rest[3 * pps:]
    j = pl.program_id(1)
    own = (_iota((A_HEADS, A_WIDTH), 1) // HEAD_DIM) == _iota((A_HEADS, A_WIDTH), 0)
    qbd = jnp.where(own, q_ref[...], 0.0)

    @pl.when(j == 0)
    def _():
        m_ref[...] = jnp.broadcast_to(jnp.sum(qbd * kn_ref[...], axis=1, keepdims=True), m_ref.shape)
        l_ref[...] = jnp.ones(l_ref.shape, F32)
        acc_ref[...] = jnp.broadcast_to(vn_ref[...], acc_ref.shape)
        suf_ref[...] = jnp.zeros(suf_ref.shape, F32)

    qb = qbd.astype(BF16)
    cn = cn_ref[...]
    m, l, acc, suf = m_ref[...], l_ref[...], acc_ref[...], suf_ref[...]

    def page_t(ref):
        return ref[...].reshape(A_WIDTH, ref.shape[-1]).astype(BF16)

    scores = []
    for u in range(pps):
        r = r_refs[u][...]
        s = (cn + suf + r[0:A_HEADS]) * LOG2E + _dot(qb, page_t(k_refs[u]))
        suf = suf + r[A_HEADS:2 * A_HEADS]
        scores.append(s)
    smax = scores[0]
    for s in scores[1:]:
        smax = jnp.maximum(smax, s)
    m_new = jnp.maximum(m, jnp.max(smax, axis=1, keepdims=True))
    a = jnp.exp2(m - m_new)
    probs = [jnp.exp2(s - m_new) for s in scores]
    psum = probs[0]
    for p in probs[1:]:
        psum = psum + p
    l = a * l + jnp.sum(psum, axis=1, keepdims=True)
    acc = acc * a[:, 0:1]
    for u in range(pps):
        acc = acc + _dot_nt(probs[u].astype(BF16), page_t(v_refs[u]))
    m_ref[...], l_ref[...], acc_ref[...], suf_ref[...] = m_new, l, acc, suf

    @pl.when(j == pl.num_programs(1) - 1)
    def _():
        o_ref[...] = jnp.sum(jnp.where(own, acc / l[:, 0:1], 0.0), axis=0, keepdims=True)


def _fox_decode(page_table, layer, q, k_new, v_new, logf_new, cache_kt, cache_vt, suffix):
    db, n_pages = page_table.shape
    pps = PAGES_PER_STEP
    assert n_pages % pps == 0
    steps = n_pages // pps
    rows = cache_kt.shape[-1]
    tok = pl.BlockSpec((None, 1, A_WIDTH), lambda b, j, pt: (b, 0, 0))

    def page(u, shape):
        zeros = (0,) * len(shape)
        return pl.BlockSpec((None, None) + shape,
                            lambda b, j, pt: (layer, pt[b, n_pages - 1 - (j * pps + u)]) + zeros)

    grid_spec = pltpu.PrefetchScalarGridSpec(
        num_scalar_prefetch=1,
        grid=(db, steps),
        in_specs=[tok, tok, tok, pl.BlockSpec((None, A_HEADS, LANES), lambda b, j, pt: (b, 0, 0))]
                 + [page(u, (A_HEADS, HEAD_DIM, rows)) for u in range(pps)]
                 + [page(u, (A_HEADS, HEAD_DIM, rows)) for u in range(pps)]
                 + [page(u, (2 * A_HEADS, rows)) for u in range(pps)],
        out_specs=tok,
        scratch_shapes=[pltpu.VMEM((A_HEADS, LANES), F32), pltpu.VMEM((A_HEADS, LANES), F32),
                        pltpu.VMEM((A_HEADS, A_WIDTH), F32), pltpu.VMEM((A_HEADS, LANES), F32)],
    )
    tok3 = lambda t: t.reshape(db, 1, A_WIDTH)
    return pl.pallas_call(
        _fox_decode_kernel,
        grid_spec=grid_spec,
        out_shape=jax.ShapeDtypeStruct((db, 1, A_WIDTH), F32),
        compiler_params=_cparams(("parallel", "arbitrary")),
        name="fox_decode",
    )(page_table, tok3(q), tok3(k_new), tok3(v_new), logf_new, *([cache_kt] * pps), *([cache_vt] * pps),
      *([suffix] * pps))


def _sample_mix_kernel(h_ref, cbuf_ref, dw_ref, dwb_ref, lng_ref, lnb_ref, pw_ref, pwb_ref,
                       x_ref, dbuf_ref, cw_ref, small_ref, alog_ref, dtb_ref,
                       ob_ref, q_ref, k_ref, v_ref, eg_ref, bt_ref, lf_ref):
    keep = CONF_WIDTH - 1
    h = h_ref[...]
    cbuf = cbuf_ref[...]
    acc = jnp.sum(cbuf * dw_ref[0:keep, :][None], axis=1) + dw_ref[keep:keep + 1, :] * h + dwb_ref[...]
    mu = jnp.mean(acc, axis=-1, keepdims=True)
    xc = acc - mu
    var = jnp.mean(xc * xc, axis=-1, keepdims=True)
    y = xc * lax.rsqrt(var + EPS) * lng_ref[...] + lnb_ref[...]
    ob_ref[...] = _dot(_silu(y).astype(BF16), pw_ref[...]) + pwb_ref[...]

    dk = DN_CONV - 1
    x = x_ref[...]
    dbuf = dbuf_ref[...]
    conv = jnp.sum(dbuf * cw_ref[0:dk, :][None], axis=1) + cw_ref[dk:dk + 1, :] * x
    yv = _silu(conv)
    qq, kk, vv = yv[:, :C_WIDTH], yv[:, C_WIDTH:2 * C_WIDTH], yv[:, 2 * C_WIDTH:]
    ones_bd = _head_sum_matrix()
    q_ref[...] = qq * lax.rsqrt(_head_sumsq(qq, ones_bd) + EPS) * (HEAD_DIM ** -0.5)
    k_ref[...] = kk * lax.rsqrt(_head_sumsq(kk, ones_bd) + EPS)
    v_ref[...] = vv
    sm = small_ref[...]
    eg_ref[...] = jnp.exp(-jnp.exp(alog_ref[...]) * _softplus(sm + dtb_ref[...]))
    bt_ref[...] = _sigmoid(sm)
    lf_ref[...] = _log_sigmoid(sm)


def _sample_mix(h, cbuf, dw, dwb, lng, lnb, pw, pwb, x, dbuf, cw, small, alog, dtb):
    db = h.shape[0]
    f = lambda *shape: jax.ShapeDtypeStruct(shape, F32)
    return pl.pallas_call(
        _sample_mix_kernel,
        out_shape=[f(db, B_WIDTH), f(db, C_WIDTH), f(db, C_WIDTH), f(db, C_WIDTH),
                   f(db, LANES), f(db, LANES), f(db, LANES)],
        compiler_params=pltpu.CompilerParams(vmem_limit_bytes=VMEM_LIMIT_BYTES),
        name="sample_mix",
    )(h, cbuf, dw, dwb, lng, lnb, pw, pwb, x, dbuf, cw, small, alog, dtb)


def _sample_delta_kernel(s_ref, qc_ref, kc_ref, qr_ref, kr_ref, v_ref, eg_ref, bt_ref, o_ref, snew_ref):
    sd = s_ref[...] * eg_ref[...]
    ks = jnp.sum(kc_ref[...] * sd, axis=1, keepdims=True)
    u = bt_ref[...] * (v_ref[...] - ks)
    qs = jnp.sum(qc_ref[...] * sd, axis=1, keepdims=True)
    qk = jnp.sum(qr_ref[...] * kr_ref[...], axis=2, keepdims=True)
    o_ref[...] = qs + qk * u
    snew_ref[...] = sd + kc_ref[...] * u


def _sample_delta(s, qn, kn, vn, eg, bt):
    db = s.shape[0]
    g = db * C_HEADS
    col = lambda t: t.reshape(g, HEAD_DIM, 1)
    row = lambda t: t.reshape(g, 1, HEAD_DIM)
    gate = lambda t, off: t[:, off:off + C_HEADS].reshape(g, 1, 1)
    o, s_new = pl.pallas_call(
        _sample_delta_kernel,
        out_shape=[jax.ShapeDtypeStruct((g, 1, HEAD_DIM), F32), jax.ShapeDtypeStruct((g, HEAD_DIM, HEAD_DIM), F32)],
        compiler_params=pltpu.CompilerParams(vmem_limit_bytes=VMEM_LIMIT_BYTES),
        name="sample_delta",
    )(s.reshape(g, HEAD_DIM, HEAD_DIM), col(qn), col(kn), row(qn), row(kn), row(vn), gate(eg, SM_A), gate(bt, SM_B))
    return o.reshape(db, C_WIDTH), s_new.reshape(db, C_HEADS, HEAD_DIM, HEAD_DIM)


def _permute_in_cols(t):
    aw, bw, cw = A_WIDTH, B_WIDTH, C_WIDTH
    f0 = 3 * aw
    glu0 = f0 + A_HEADS
    qkv0 = glu0 + 2 * bw
    a0 = qkv0 + 3 * cw
    b0 = a0 + C_HEADS
    z0 = b0 + C_HEADS
    parts = [t[..., 0:f0], t[..., glu0:qkv0], t[..., qkv0:a0], t[..., z0:z0 + cw],
             t[..., f0:glu0], t[..., a0:b0], t[..., b0:z0]]
    out = jnp.concatenate(parts, axis=-1)
    return jnp.pad(out, [(0, 0)] * (t.ndim - 1) + [(0, IN_COLS_PAD - out.shape[-1])])


def _lane_vec(vals, off):
    return jnp.zeros((1, LANES), F32).at[0, off:off + vals.shape[0]].set(vals)


def _row_tile(n):
    for tm in (384, 256, 128, 64, 32, 16, 8):
        if n % tm == 0:
            return tm
    raise ValueError(n)


def kernel(x_prompt, x_sample, cache_k, cache_v, cache_logf, state_conf_conv, state_dn_conv, state_dn_rec, page_table, meta_tokens, norm_mix, w_in, b_in, conf_dw, conf_dw_b, conf_ln_g, conf_ln_b, conf_pw, conf_pw_b, dn_conv, dn_a_log, dn_dt_bias, dn_norm_g, grp_norm_a, grp_norm_b, w_out, norm_ffn, w_ffn_gate, w_ffn_up, w_ffn_down, final_norm):
    batch, seq_in, d = x_prompt.shape
    db = x_sample.shape[0]
    assert x_sample.shape[1] == 1
    depth = w_in.shape[0]
    seq = NUM_META + seq_in
    n = batch * seq

    xp = jnp.concatenate([jnp.broadcast_to(meta_tokens[None], (batch, NUM_META, d)), x_prompt], axis=1).reshape(n, d)
    xs = x_sample.reshape(db, d)
    w_in_p = _permute_in_cols(w_in.astype(BF16))
    b_in_p = _permute_in_cols(b_in)
    tm_p, tm_s = _row_tile(n), _row_tile(db)
    fin = final_norm.reshape(1, d)
    r1 = lambda t: t.reshape(1, -1)
    suffix = _pool_suffix(jnp.transpose(cache_logf, (0, 1, 3, 2)))
    cache_kt = jnp.transpose(cache_k, (0, 1, 3, 4, 2))
    cache_vt = jnp.transpose(cache_v, (0, 1, 3, 4, 2))

    p_out = [[] for _ in range(6)]
    s_out = [[] for _ in range(6)]
    for l in range(depth):
        g_mix, g_ffn = r1(norm_mix[l]), r1(norm_ffn[l])
        wo, wg, wu, wd = (w_out[l].astype(BF16), w_ffn_gate[l].astype(BF16), w_ffn_up[l].astype(BF16),
                          w_ffn_down[l].astype(BF16))
        pw = conf_pw[l].astype(BF16)
        alog, dtb = _lane_vec(dn_a_log[l], SM_A), _lane_vec(dn_dt_bias[l], SM_A)
        ng_t = jnp.tile(dn_norm_g[l], C_HEADS).reshape(1, C_WIDTH)
        last = l == depth - 1
        conf_args = (conf_dw[l], r1(conf_dw_b[l]), r1(conf_ln_g[l]), r1(conf_ln_b[l]), pw, r1(conf_pw_b[l]))
        ffn_args = (r1(grp_norm_a[l]), r1(grp_norm_b[l]), ng_t, wo, g_ffn, wg, wu, wd, fin)

        qb, kb, vb, _, kf, vf, h, qkvc, z, small = _inproj(xp, g_mix, w_in_p[l], b_in_p[l:l + 1], tm_p)
        kf, vf = (t.reshape(batch, seq, A_HEADS, HEAD_DIM) for t in (kf, vf))
        logf, qaug, kaug = _fox_gates(small, batch, seq)
        oa = _fox_prompt(qb, kb, vb, qaug, kaug, batch, seq)
        ob = _conformer(h, *conf_args, batch, seq)
        conf_buf = h.reshape(batch, seq, B_WIDTH)[:, seq - (CONF_WIDTH - 1):]
        dn_buf = qkvc.reshape(batch, seq, 3 * C_WIDTH)[:, seq - (DN_CONV - 1):]
        oc, s_rec = _deltanet(qkvc, small, dn_conv[l], alog, dtb, batch, seq)
        if last:
            y_prompt = _out_ffn_last(xp, oa, ob, oc, z, *ffn_args, batch, seq)
        else:
            xp = _out_ffn(xp, oa, ob, oc, z, *ffn_args, tm_p, False)
        for i, t in enumerate((kf, vf, logf, conf_buf, dn_buf, s_rec)):
            p_out[i].append(t)

        _, _, _, qf, kf, vf, h, qkvc, z, small = _inproj(xs, g_mix, w_in_p[l], b_in_p[l:l + 1], tm_s)
        conf_new = jnp.concatenate([state_conf_conv[l][:, 1:], h[:, None]], axis=1)
        dn_new = jnp.concatenate([state_dn_conv[l][:, 1:], qkvc[:, None]], axis=1)
        ob, qn, kn, vn, eg, bt, lf = _sample_mix(
            h, state_conf_conv[l], *conf_args, qkvc, state_dn_conv[l], dn_conv[l], small, alog, dtb)
        lf_heads = lf[:, SM_F:SM_F + A_HEADS]
        cn = jnp.broadcast_to(lf_heads[:, :, None], (db, A_HEADS, LANES))
        oa = _fox_decode(page_table, l, qf, kf, vf, cn, cache_kt, cache_vt, suffix)
        oa = jnp.transpose(oa.reshape(db, A_PAIRS, LANES), (1, 0, 2))
        oc, s_new = _sample_delta(state_dn_rec[l], qn, kn, vn, eg, bt)
        xs = _out_ffn(xs, oa, ob, oc, z, *ffn_args, tm_s, last)
        for i, t in enumerate((kf.reshape(db, 1, A_HEADS, HEAD_DIM), vf.reshape(db, 1, A_HEADS, HEAD_DIM),
                               lf_heads.reshape(db, 1, A_HEADS), conf_new, dn_new, s_new)):
            s_out[i].append(t)

    y_sample = xs.reshape(db, 1, d)
    return (y_prompt, y_sample, *[jnp.stack(a) for a in p_out], *[jnp.stack(a) for a in s_out])
```
